```python
import jax, jax.numpy as jnp
from jax import lax
import numpy as np

D_MODEL = 1024
BATCH = 4
SEQ = 4096
DEPTH = 4

GRID_W = 64
CTX_LEN = 256
CHUNK = 128
A_GROUPS = 4
A_W = D_MODEL // 2
A_GROUP_DIM = A_W // A_GROUPS
B_HEAD_DIM = 64
B_W = D_MODEL // 2
B_HEADS = B_W // B_HEAD_DIM
WIN_H = 8
WIN_W = 16
MIX_IN = 2 * A_W + 3 * B_W
MIX_OUT = A_W + B_W
CONV_K = 31
CONV_W = D_MODEL
FFN = 4 * D_MODEL
N_EVEN = (DEPTH + 1) // 2
N_ODD = DEPTH // 2
EPS = 1e-6
NEG = -1e30

kernel_name = "hybrid_gmlp_natten_conformer_dit"


def rmsnorm(x, g):
    x32 = x.astype(jnp.float32)
    y = x32 * lax.rsqrt(jnp.mean(x32 * x32, axis=-1, keepdims=True) + EPS)
    return (y * g.astype(jnp.float32)).astype(x.dtype)


def layernorm(x, g, b=None):
    x32 = x.astype(jnp.float32)
    mu = jnp.mean(x32, axis=-1, keepdims=True)
    var = jnp.mean(jnp.square(x32 - mu), axis=-1, keepdims=True)
    y = (x32 - mu) * lax.rsqrt(var + EPS) * g.astype(jnp.float32)
    if b is not None:
        y = y + b.astype(jnp.float32)
    return y.astype(x.dtype)


def modulate(h, shift, scale):
    return h * (1 + scale) + shift


def split_heads(t):
    return t.reshape(*t.shape[:-1], B_HEADS, B_HEAD_DIM)


def spatial_gating(z, ln_v_g, w_sp, b_sp):
    u, g = jnp.split(jax.nn.gelu(z), 2, axis=-1)
    bsz, length, _ = g.shape
    g = g.reshape(bsz, length // CHUNK, CHUNK, A_GROUPS, A_GROUP_DIM)
    g = layernorm(g, ln_v_g.reshape(A_GROUPS, A_GROUP_DIM))
    s = jnp.einsum('gpq,bnqgc->bnpgc', w_sp, g) + b_sp.T[:, :, None]
    return u * s.reshape(bsz, length, A_W)


def neighborhood_attention(q, k, v, k_ctx, v_ctx, rpb):
    bsz, seq, nh, dh = q.shape
    rows_n = seq // GRID_W
    kh = min(WIN_H, rows_n)
    kw = WIN_W
    qg = q.reshape(bsz, rows_n, GRID_W, nh, dh)
    kg = k.reshape(bsz, rows_n, GRID_W, nh, dh)
    vg = v.reshape(bsz, rows_n, GRID_W, nh, dh)
    rows = jnp.arange(rows_n)
    row_idx = jnp.clip(rows - kh // 2, 0, rows_n - kh)[:, None] + jnp.arange(kh)
    k_band = kg[:, row_idx]
    v_band = vg[:, row_idx]
    cols = jnp.arange(GRID_W)
    col_start = jnp.clip(cols - kw // 2, 0, GRID_W - kw)
    col_mask = (cols[None, :] >= col_start[:, None]) & (cols[None, :] < col_start[:, None] + kw)
    dr = row_idx - rows[:, None] + WIN_H - 1
    dc = jnp.clip(cols[None, :] - cols[:, None] + WIN_W - 1, 0, 2 * WIN_W - 2)
    bias = rpb[:, dr[:, None, :, None], dc[None, :, None, :]].astype(jnp.float32)
    scale = dh ** -0.5
    s_loc = jnp.einsum('brqhd,brkwhd->bhrqkw', qg, k_band).astype(jnp.float32) * scale + bias[None]
    s_loc = jnp.where(col_mask[:, None, :], s_loc, NEG)
    s_ctx = jnp.einsum('brqhd,bchd->bhrqc', qg, k_ctx).astype(jnp.float32) * scale
    n_loc = kh * GRID_W
    s = jnp.concatenate([s_loc.reshape(bsz, nh, rows_n, GRID_W, n_loc), s_ctx], axis=-1)
    p = jax.nn.softmax(s, axis=-1).astype(v.dtype)
    p_loc = p[..., :n_loc].reshape(bsz, nh, rows_n, GRID_W, kh, GRID_W)
    p_ctx = p[..., n_loc:]
    out = (jnp.einsum('bhrqkw,brkwhd->brqhd', p_loc, v_band)
           + jnp.einsum('bhrqc,bchd->brqhd', p_ctx, v_ctx))
    return out.reshape(bsz, seq, nh * dh)


def context_attention(q, k, v):
    s = jnp.einsum('bqhd,bkhd->bhqk', q, k).astype(jnp.float32) * (q.shape[-1] ** -0.5)
    p = jax.nn.softmax(s, axis=-1).astype(v.dtype)
    out = jnp.einsum('bhqk,bkhd->bqhd', p, v)
    return out.reshape(*out.shape[:2], -1)


def even_mixer(h, hc, w_in, w_out, ln_v_g, w_sp, b_sp, rpb, ctx_out):
    cuts = [2 * A_W, 2 * A_W + B_W, 2 * A_W + 2 * B_W]
    if ctx_out:
        zc, qc, kc, vc = jnp.split(hc @ w_in, cuts, axis=-1)
    else:
        kc, vc = jnp.split(hc @ w_in[:, cuts[1]:], 2, axis=-1)
    kc, vc = split_heads(kc), split_heads(vc)
    z, q, k, v = jnp.split(h @ w_in, cuts, axis=-1)
    a = spatial_gating(z, ln_v_g, w_sp, b_sp)
    b = neighborhood_attention(split_heads(q), split_heads(k), split_heads(v), kc, vc, rpb)
    y = jnp.concatenate([a, b], axis=-1) @ w_out
    yc = None
    if ctx_out:
        ac = spatial_gating(zc, ln_v_g, w_sp, b_sp)
        bc = context_attention(split_heads(qc), kc, vc)
        yc = jnp.concatenate([ac, bc], axis=-1) @ w_out
    return y, yc


def conformer_conv(h, w_pw1, b_pw1, w_dw, b_dw, ln_g, ln_b, w_pw2, b_pw2):
    a, gate = jnp.split(h @ w_pw1 + b_pw1, 2, axis=-1)
    y = a * jax.nn.sigmoid(gate)
    y = lax.conv_general_dilated(y, w_dw[:, None, :], window_strides=(1,),
                                 padding=[(CONV_K // 2, CONV_K // 2)],
                                 dimension_numbers=('NWC', 'WIO', 'NWC'),
                                 feature_group_count=CONV_W) + b_dw
    y = jax.nn.silu(layernorm(y, ln_g, ln_b))
    return y @ w_pw2 + b_pw2


def sq_relu_mlp(h, w1, w2):
    return jnp.square(jax.nn.relu(h @ w1)) @ w2


def setup_inputs(seed: int = 0) -> dict:
    key = jax.random.key(seed)
    ks = jax.random.split(key, 24)
    f32 = jnp.float32
    nrm = lambda k, shape, s: jax.random.normal(k, shape, f32) * s
    D = D_MODEL
    return {
        "x": nrm(ks[0], (BATCH, SEQ, D), 1.0),
        "c": nrm(ks[1], (BATCH, D), 1.0),
        "ctx": nrm(ks[2], (BATCH, CTX_LEN, D), 1.0),
        "c_ctx": nrm(ks[3], (D,), 1.0),
        "w_mod": nrm(ks[4], (DEPTH, D, 6 * D), 0.5 * D ** -0.5),
        "b_mod": nrm(ks[5], (DEPTH, 6 * D), 0.02),
        "norm_g": 1.0 + nrm(ks[6], (DEPTH, 2, D), 0.05),
        "w_in": nrm(ks[7], (N_EVEN, D, MIX_IN), D ** -0.5),
        "w_out": nrm(ks[8], (N_EVEN, MIX_OUT, D), MIX_OUT ** -0.5),
        "ln_v_g": 1.0 + nrm(ks[9], (N_EVEN, A_W), 0.05),
        "w_sp": nrm(ks[10], (N_EVEN, A_GROUPS, CHUNK, CHUNK), CHUNK ** -0.5),
        "b_sp": 1.0 + nrm(ks[11], (N_EVEN, A_GROUPS, CHUNK), 0.1),
        "rpb": nrm(ks[12], (N_EVEN, B_HEADS, 2 * WIN_H - 1, 2 * WIN_W - 1), 0.1),
        "w_pw1": nrm(ks[13], (N_ODD, D, 2 * CONV_W), D ** -0.5),
        "b_pw1": nrm(ks[14], (N_ODD, 2 * CONV_W), 0.02),
        "w_dw": nrm(ks[15], (N_ODD, CONV_K, CONV_W), CONV_K ** -0.5),
        "b_dw": nrm(ks[16], (N_ODD, CONV_W), 0.02),
        "ln_c_g": 1.0 + nrm(ks[17], (N_ODD, CONV_W), 0.05),
        "ln_c_b": nrm(ks[18], (N_ODD, CONV_W), 0.02),
        "w_pw2": nrm(ks[19], (N_ODD, CONV_W, D), CONV_W ** -0.5),
        "b_pw2": nrm(ks[20], (N_ODD, D), 0.02),
        "w_ff1": nrm(ks[21], (DEPTH, D, FFN), D ** -0.5),
        "w_ff2": nrm(ks[22], (DEPTH, FFN, D), FFN ** -0.5),
        "final_g": 1.0 + nrm(ks[23], (D,), 0.05),
    }


def reference(x, c, ctx, c_ctx, w_mod, b_mod, norm_g, w_in, w_out, ln_v_g, w_sp, b_sp, rpb,
              w_pw1, b_pw1, w_dw, b_dw, ln_c_g, ln_c_b, w_pw2, b_pw2, w_ff1, w_ff2, final_g):
    last_reader = ((DEPTH - 1) // 2) * 2
    sc_c = jax.nn.silu(c)
    sc_ctx = jax.nn.silu(c_ctx)
    for l in range(DEPTH):
        m = (sc_c @ w_mod[l] + b_mod[l])[:, None, :]
        sh1, s1, g1, sh2, s2, g2 = jnp.split(m, 6, axis=-1)
        mc = sc_ctx @ w_mod[l] + b_mod[l]
        csh1, cs1, cg1, csh2, cs2, cg2 = jnp.split(mc, 6, axis=-1)
        ctx_out = l < last_reader
        h = modulate(rmsnorm(x, norm_g[l, 0]), sh1, s1)
        if l % 2 == 0:
            i = l // 2
            hc = modulate(rmsnorm(ctx, norm_g[l, 0]), csh1, cs1)
            y, yc = even_mixer(h, hc, w_in[i], w_out[i], ln_v_g[i], w_sp[i], b_sp[i], rpb[i], ctx_out)
        else:
            i = l // 2
            conv = lambda t: conformer_conv(t, w_pw1[i], b_pw1[i], w_dw[i], b_dw[i],
                                            ln_c_g[i], ln_c_b[i], w_pw2[i], b_pw2[i])
            y = conv(h)
            yc = conv(modulate(rmsnorm(ctx, norm_g[l, 0]), csh1, cs1)) if ctx_out else None
        x = x + g1 * y
        x = x + g2 * sq_relu_mlp(modulate(rmsnorm(x, norm_g[l, 1]), sh2, s2), w_ff1[l], w_ff2[l])
        if ctx_out:
            ctx = ctx + cg1 * yc
            ctx = ctx + cg2 * sq_relu_mlp(modulate(rmsnorm(ctx, norm_g[l, 1]), csh2, cs2), w_ff1[l], w_ff2[l])
    return rmsnorm(x, final_g)
```

```python
import functools

import jax
import jax.numpy as jnp
from jax import lax
from jax.experimental import pallas as pl
from jax.experimental.pallas import tpu as pltpu

D = 1024
BATCH = 4
SEQ = 4096
DEPTH = 4
GRID_W = 64
ROWS = SEQ // GRID_W
CTX_LEN = 256
CHUNK = 128
A_GROUPS = 4
A_W = 512
B_W = 512
HEAD_DIM = 64
HEADS = 8
WIN_H = 8
WIN_W = 16
MIX_IN = 2 * A_W + 3 * B_W
CONV_K = 31
HALO = 16
FFN = 4 * D
FFN_CHUNK = 1024
EPS = 1e-6
NEG = -1e30

T_LAT = BATCH * SEQ
T_CTX = BATCH * CTX_LEN
T_ALL = T_LAT + T_CTX
TILE = 256
TILES_PER_SEQ = SEQ // TILE
LAT_TILES = T_LAT // TILE
MOD_ROWS = 8
MOD_N = 768
VMEM_LIMIT = 56 * 1024 * 1024

BF16 = jnp.bfloat16
F32 = jnp.float32


def _const_spec(shape):
    zeros = (0,) * len(shape)
    return pl.BlockSpec(shape, lambda *_: zeros, pipeline_mode=pl.Buffered(1))


def _params(sem):
    return pltpu.CompilerParams(dimension_semantics=sem, vmem_limit_bytes=VMEM_LIMIT)


def _mod_row_spec(layer):
    return pl.BlockSpec(
        (None, 1, 6 * D),
        lambda i: (layer * MOD_ROWS + jnp.minimum(i // TILES_PER_SEQ, BATCH), 0, 0))


def _rms_mod(x, g, shift, scale):
    y = x * lax.rsqrt(jnp.mean(x * x, axis=-1, keepdims=True) + EPS) * g
    return y * (1.0 + scale) + shift


def _dot(a, b):
    return jnp.dot(a, b, preferred_element_type=F32)


def _dot_nt(a, b):
    return lax.dot_general(a, b, (((1,), (1,)), ((), ())), preferred_element_type=F32)


def _mod_kernel(cc_ref, w_ref, b_ref, o_ref):
    cc = cc_ref[...]
    sc = (cc * jax.nn.sigmoid(cc)).astype(BF16)
    o_ref[...] = _dot(sc, w_ref[...].astype(BF16)) + b_ref[...]


def _modulation(cc, w_mod, b_mod):
    nblk = (6 * D) // MOD_N
    return pl.pallas_call(
        _mod_kernel,
        grid=(DEPTH, nblk),
        in_specs=[
            pl.BlockSpec((MOD_ROWS, D), lambda l, n: (0, 0)),
            pl.BlockSpec((None, D, MOD_N), lambda l, n: (l, 0, n)),
            pl.BlockSpec((None, 1, MOD_N), lambda l, n: (l, 0, n)),
        ],
        out_specs=pl.BlockSpec((None, MOD_ROWS, MOD_N), lambda l, n: (l, 0, n)),
        out_shape=jax.ShapeDtypeStruct((DEPTH, MOD_ROWS, 6 * D), F32),
        compiler_params=_params(("arbitrary", "arbitrary")),
    )(cc, w_mod, b_mod.reshape(DEPTH, 1, 6 * D))


def _even_in_kernel(x_ref, mod_ref, ng_ref, w_in_ref, lng_ref, w_sp_ref, b_sp_ref,
                    a_ref, q_ref, k_ref, v_ref):
    h = _rms_mod(x_ref[...], ng_ref[...], mod_ref[:, 0:D], mod_ref[:, D:2 * D]).astype(BF16)
    z = jax.nn.gelu(_dot(h, w_in_ref[:, 0:2 * A_W]))
    gd = A_W // A_GROUPS
    for grp in range(A_GROUPS):
        g = z[:, A_W + grp * gd:A_W + (grp + 1) * gd]
        mu = jnp.mean(g, axis=-1, keepdims=True)
        gc = g - mu
        var = jnp.mean(gc * gc, axis=-1, keepdims=True)
        gn = (gc * lax.rsqrt(var + EPS) * lng_ref[:, grp * gd:(grp + 1) * gd]).astype(BF16)
        for ch in range(TILE // CHUNK):
            rows = slice(ch * CHUNK, (ch + 1) * CHUNK)
            s = _dot(w_sp_ref[grp], gn[rows]) + b_sp_ref[grp]
            a_ref[rows, grp * gd:(grp + 1) * gd] = (z[rows, grp * gd:(grp + 1) * gd] * s).astype(BF16)
    c0 = 2 * A_W
    q_ref[...] = (_dot(h, w_in_ref[:, c0:c0 + B_W]) * (HEAD_DIM ** -0.5)).astype(BF16)
    k_ref[...] = _dot(h, w_in_ref[:, c0 + B_W:c0 + 2 * B_W]).astype(BF16)
    v_ref[...] = _dot(h, w_in_ref[:, c0 + 2 * B_W:c0 + 3 * B_W]).astype(BF16)


def _even_in(xs, mod3, layer, ng, w_in, lng, w_sp, b_sp):
    ntiles = T_ALL // TILE
    tok = lambda w: pl.BlockSpec((TILE, w), lambda i: (i, 0))
    out = jax.ShapeDtypeStruct((T_ALL, A_W), BF16)
    return pl.pallas_call(
        _even_in_kernel,
        grid=(ntiles,),
        in_specs=[
            tok(D), _mod_row_spec(layer), _const_spec((1, D)), _const_spec((D, MIX_IN)),
            _const_spec((1, A_W)), _const_spec((A_GROUPS, CHUNK, CHUNK)),
            _const_spec((A_GROUPS, CHUNK, 1)),
        ],
        out_specs=[tok(A_W)] * 4,
        out_shape=[out] * 4,
        compiler_params=_params(("arbitrary",)),
    )(xs, mod3, ng, w_in, lng, w_sp, b_sp)


def _attn_kernel(q_ref, k_ref, v_ref, kc_ref, vc_ref, bias_ref, o_ref, *, ctx_queries):
    r = pl.program_id(1)
    lane = lax.broadcasted_iota(jnp.int32, (GRID_W, 2 * HEAD_DIM), 1)
    first_head = lane < HEAD_DIM

    def attend(local):
        if local:
            kh = WIN_H * GRID_W
            start = pl.multiple_of(jnp.clip(r - WIN_H // 2, 0, ROWS - WIN_H) * GRID_W, GRID_W)
            cq = lax.broadcasted_iota(jnp.int32, (GRID_W, kh), 0)
            ck = lax.broadcasted_iota(jnp.int32, (GRID_W, kh), 1) & (GRID_W - 1)
            cs = jnp.clip(cq - WIN_W // 2, 0, GRID_W - WIN_W)
            col_ok = (ck >= cs) & (ck < cs + WIN_W)
        for hp in range(HEADS // 2):
            cols = slice(hp * 2 * HEAD_DIM, (hp + 1) * 2 * HEAD_DIM)
            q2 = q_ref[:, cols]
            kc2 = kc_ref[:, cols]
            vc2 = vc_ref[:, cols]
            if local:
                k2 = k_ref[pl.ds(start, kh), cols]
                v2 = v_ref[pl.ds(start, kh), cols]
            outs = []
            for half in range(2):
                qm = jnp.where(first_head if half == 0 else ~first_head, q2, jnp.zeros_like(q2))
                s_c = _dot_nt(qm, kc2)
                m = jnp.max(s_c, axis=-1, keepdims=True)
                if local:
                    s_l = jnp.where(col_ok, _dot_nt(qm, k2) + bias_ref[2 * hp + half], NEG)
                    m = jnp.maximum(m, jnp.max(s_l, axis=-1, keepdims=True))
                p_c = jnp.exp(s_c - m)
                den = jnp.sum(p_c, axis=-1, keepdims=True)
                o = _dot(p_c.astype(BF16), vc2)
                if local:
                    p_l = jnp.exp(s_l - m)
                    den = den + jnp.sum(p_l, axis=-1, keepdims=True)
                    o = o + _dot(p_l.astype(BF16), v2)
                outs.append(o / den)
            o_ref[:, cols] = jnp.where(first_head, outs[0], outs[1]).astype(BF16)

    if ctx_queries:
        pl.when(r < ROWS)(lambda: attend(True))
        pl.when(r >= ROWS)(lambda: attend(False))
    else:
        attend(True)


def _attention(q, k, v, bias, ctx_queries):
    ctx_steps = CTX_LEN // GRID_W
    steps = ROWS + (ctx_steps if ctx_queries else 0)
    lat_blocks = T_LAT // GRID_W

    def q_idx(b, r):
        return (jnp.where(r < ROWS, b * ROWS + r, lat_blocks + b * ctx_steps + (r - ROWS)), 0)

    def bias_idx(b, r):
        off = jnp.clip(r - WIN_H // 2, 0, ROWS - WIN_H) - r + WIN_H - 1
        return (jnp.where(r < ROWS, off, 0), 0, 0, 0)

    ctx_blk = T_LAT // CTX_LEN
    return pl.pallas_call(
        functools.partial(_attn_kernel, ctx_queries=ctx_queries),
        grid=(BATCH, steps),
        in_specs=[
            pl.BlockSpec((GRID_W, B_W), q_idx),
            pl.BlockSpec((SEQ, B_W), lambda b, r: (b, 0)),
            pl.BlockSpec((SEQ, B_W), lambda b, r: (b, 0)),
            pl.BlockSpec((CTX_LEN, B_W), lambda b, r: (ctx_blk + b, 0)),
            pl.BlockSpec((CTX_LEN, B_W), lambda b, r: (ctx_blk + b, 0)),
            pl.BlockSpec((None, HEADS, GRID_W, WIN_H * GRID_W), bias_idx),
        ],
        out_specs=pl.BlockSpec((GRID_W, B_W), q_idx),
        out_shape=jax.ShapeDtypeStruct((T_ALL if ctx_queries else T_LAT, B_W), BF16),
        compiler_params=_params(("arbitrary", "arbitrary")),
    )(q, k, v, k, v, bias)


def _bias_tables(rpb):
    cols = jnp.arange(GRID_W)
    dc = jnp.clip(cols[None, :] - cols[:, None] + WIN_W - 1, 0, 2 * WIN_W - 2)
    tbl = rpb[:, :, dc]
    var = jnp.stack([tbl[:, o:o + WIN_H] for o in range(WIN_H)])
    return var.transpose(0, 1, 3, 2, 4).reshape(WIN_H, HEADS, GRID_W, WIN_H * GRID_W)


def _ffn_tail(x, y, mod_ref, ng2_ref, w1_ref, w2_ref, fg_ref):
    x1 = x + mod_ref[:, 2 * D:3 * D] * y
    h2 = _rms_mod(x1, ng2_ref[...], mod_ref[:, 3 * D:4 * D], mod_ref[:, 4 * D:5 * D]).astype(BF16)
    acc = jnp.zeros_like(x1)
    for c in range(FFN // FFN_CHUNK):
        cs = slice(c * FFN_CHUNK, (c + 1) * FFN_CHUNK)
        hid = jnp.maximum(_dot(h2, w1_ref[:, cs]), 0.0)
        acc = acc + _dot((hid * hid).astype(BF16), w2_ref[cs, :])
    x2 = x1 + mod_ref[:, 5 * D:6 * D] * acc
    if fg_ref is not None:
        x2 = x2 * lax.rsqrt(jnp.mean(x2 * x2, axis=-1, keepdims=True) + EPS) * fg_ref[...]
    return x2


def _even_out_kernel(*refs, final):
    if final:
        a_ref, b_ref, x_ref, mod_ref, ng2_ref, wo_ref, w1_ref, w2_ref, fg_ref, o_ref = refs
    else:
        a_ref, b_ref, x_ref, mod_ref, ng2_ref, wo_ref, w1_ref, w2_ref, o_ref = refs
        fg_ref = None
    y = _dot(a_ref[...], wo_ref[0:A_W, :]) + _dot(b_ref[...], wo_ref[A_W:A_W + B_W, :])
    o_ref[...] = _ffn_tail(x_ref[...], y, mod_ref, ng2_ref, w1_ref, w2_ref, fg_ref)


def _even_out(a, b, xs, mod3, layer, ng2, w_out, w1, w2, n_out, final_g=None):
    tok = lambda w: pl.BlockSpec((TILE, w), lambda i: (i, 0))
    in_specs = [tok(A_W), tok(B_W), tok(D), _mod_row_spec(layer), _const_spec((1, D)),
                _const_spec((A_W + B_W, D)), _const_spec((D, FFN)), _const_spec((FFN, D))]
    args = [a, b, xs, mod3, ng2, w_out, w1, w2]
    if final_g is not None:
        in_specs.append(_const_spec((1, D)))
        args.append(final_g)
    return pl.pallas_call(
        functools.partial(_even_out_kernel, final=final_g is not None),
        grid=(n_out // TILE,),
        in_specs=in_specs,
        out_specs=tok(D),
        out_shape=jax.ShapeDtypeStruct((n_out, D), F32),
        compiler_params=_params(("arbitrary",)),
    )(*args)


def _odd_kernel(*refs, final):
    if final:
        (xp_ref, x_ref, xn_ref, mod_ref, ng1_ref, ng2_ref, wp1_ref, bp1_ref, wdw_ref, bdw_ref,
         lg_ref, lb_ref, wp2_ref, bp2_ref, w1_ref, w2_ref, fg_ref, o_ref, y_scr) = refs
    else:
        (xp_ref, x_ref, xn_ref, mod_ref, ng1_ref, ng2_ref, wp1_ref, bp1_ref, wdw_ref, bdw_ref,
         lg_ref, lb_ref, wp2_ref, bp2_ref, w1_ref, w2_ref, o_ref, y_scr) = refs
        fg_ref = None
    i = pl.program_id(0)
    is_lat = i < LAT_TILES
    pos = i % TILES_PER_SEQ
    seq_first = jnp.logical_or(jnp.logical_not(is_lat), pos == 0)
    seq_last = jnp.logical_or(jnp.logical_not(is_lat), pos == TILES_PER_SEQ - 1)

    x = x_ref[...]
    xe = jnp.concatenate([xp_ref[...], x, xn_ref[...]], axis=0)
    h = _rms_mod(xe, ng1_ref[...], mod_ref[:, 0:D], mod_ref[:, D:2 * D]).astype(BF16)
    t = _dot(h, wp1_ref[...]) + bp1_ref[...]
    glu = t[:, 0:D] * jax.nn.sigmoid(t[:, D:2 * D])
    row = lax.broadcasted_iota(jnp.int32, (HALO + TILE + HALO, 1), 0)
    pad = jnp.logical_or(jnp.logical_and(seq_first, row < HALO),
                         jnp.logical_and(seq_last, row >= HALO + TILE))
    y_scr[...] = jnp.where(pad, 0.0, glu)

    acc = jnp.zeros((TILE, D), F32) + bdw_ref[...]
    base = HALO - CONV_K // 2
    for k in range(CONV_K):
        acc = acc + y_scr[pl.ds(base + k, TILE), :] * wdw_ref[k:k + 1, :]
    mu = jnp.mean(acc, axis=-1, keepdims=True)
    ac = acc - mu
    var = jnp.mean(ac * ac, axis=-1, keepdims=True)
    ln = ac * lax.rsqrt(var + EPS) * lg_ref[...] + lb_ref[...]
    act = (ln * jax.nn.sigmoid(ln)).astype(BF16)
    y = _dot(act, wp2_ref[...]) + bp2_ref[...]
    o_ref[...] = _ffn_tail(x, y, mod_ref, ng2_ref, w1_ref, w2_ref, fg_ref)


def _odd_layer(xs, mod3, layer, ng1, ng2, wp1, bp1, wdw, bdw, lg, lb, wp2, bp2, w1, w2, final_g=None):
    n = xs.shape[0]
    per = TILE // HALO
    halo_blocks = n // HALO
    in_specs = [
        pl.BlockSpec((HALO, D), lambda i: (jnp.maximum(i * per - 1, 0), 0)),
        pl.BlockSpec((TILE, D), lambda i: (i, 0)),
        pl.BlockSpec((HALO, D), lambda i: (jnp.minimum((i + 1) * per, halo_blocks - 1), 0)),
        _mod_row_spec(layer), _const_spec((1, D)), _const_spec((1, D)),
        _const_spec((D, 2 * D)), _const_spec((1, 2 * D)), _const_spec((CONV_K, D)), _const_spec((1, D)),
        _const_spec((1, D)), _const_spec((1, D)), _const_spec((D, D)), _const_spec((1, D)),
        _const_spec((D, FFN)), _const_spec((FFN, D)),
    ]
    args = [xs, xs, xs, mod3, ng1, ng2, wp1, bp1, wdw, bdw, lg, lb, wp2, bp2, w1, w2]
    if final_g is not None:
        in_specs.append(_const_spec((1, D)))
        args.append(final_g)
    return pl.pallas_call(
        functools.partial(_odd_kernel, final=final_g is not None),
        grid=(n // TILE,),
        in_specs=in_specs,
        out_specs=pl.BlockSpec((TILE, D), lambda i: (i, 0)),
        out_shape=jax.ShapeDtypeStruct((n, D), F32),
        scratch_shapes=[pltpu.VMEM((HALO + TILE + HALO, D), F32)],
        compiler_params=_params(("arbitrary",)),
    )(*args)


def kernel(x, c, ctx, c_ctx, w_mod, b_mod, norm_g, w_in, w_out, ln_v_g, w_sp, b_sp, rpb, w_pw1, b_pw1,
           w_dw, b_dw, ln_c_g, ln_c_b, w_pw2, b_pw2, w_ff1, w_ff2, final_g):
    row = lambda t: t.reshape(1, -1)
    cc = jnp.concatenate([c, c_ctx[None, :], jnp.zeros((MOD_ROWS - BATCH - 1, D), F32)], axis=0)
    mod3 = _modulation(cc, w_mod, b_mod).reshape(DEPTH * MOD_ROWS, 1, 6 * D)
    xs = jnp.concatenate([x.reshape(T_LAT, D), ctx.reshape(T_CTX, D)], axis=0)
    last_reader = ((DEPTH - 1) // 2) * 2
    for l in range(DEPTH):
        ctx_out = l < last_reader
        fg = row(final_g) if l == DEPTH - 1 else None
        w1 = w_ff1[l].astype(BF16)
        w2 = w_ff2[l].astype(BF16)
        j = l // 2
        if l % 2 == 0:
            a, q, k, v = _even_in(xs, mod3, l, row(norm_g[l, 0]), w_in[j].astype(BF16), row(ln_v_g[j]),
                                  w_sp[j].astype(BF16), b_sp[j][:, :, None])
            b = _attention(q, k, v, _bias_tables(rpb[j]), ctx_out)
            xs = _even_out(a, b, xs, mod3, l, row(norm_g[l, 1]), w_out[j].astype(BF16), w1, w2,
                           T_ALL if ctx_out else T_LAT, fg)
        else:
            if not ctx_out and xs.shape[0] != T_LAT:
                xs = xs[:T_LAT]
            xs = _odd_layer(xs, mod3, l, row(norm_g[l, 0]), row(norm_g[l, 1]), w_pw1[j].astype(BF16),
                            row(b_pw1[j]), w_dw[j], row(b_dw[j]), row(ln_c_g[j]), row(ln_c_b[j]),
                            w_pw2[j].astype(BF16), row(b_pw2[j]), w1, w2, fg)
    return xs[:T_LAT].reshape(BATCH, SEQ, D)
```

```python
import functools

import jax
import jax.numpy as jnp
import numpy as np
from jax import lax
from jax.experimental import pallas as pl
from jax.experimental.pallas import tpu as pltpu

D = 1024
BATCH = 4
SEQ = 4096
DEPTH = 4
GRID_W = 64
ROWS = SEQ // GRID_W
CTX_LEN = 256
CHUNK = 128
A_GROUPS = 4
A_W = 512
B_W = 512
HEAD_DIM = 64
HEADS = 8
WIN_H = 8
WIN_W = 16
Q_ROWS = 4
Q_BLK = Q_ROWS * GRID_W
K_ROWS = Q_ROWS + WIN_H
K_BLK = K_ROWS * GRID_W
ROW_STEPS = ROWS // Q_ROWS
MIX_IN = 2 * A_W + 3 * B_W
CONV_K = 31
HALO = 16
SUBLANES = 8
LANES = 128
CONV_ROWS = 128
FFN = 4 * D
FFN_CHUNK = 1024
EPS = 1e-6
NEG = -1e30

T_LAT = BATCH * SEQ
T_CTX = BATCH * CTX_LEN
T_ALL = T_LAT + T_CTX
TILE = 256
TILES_PER_SEQ = SEQ // TILE
LAT_TILES = T_LAT // TILE
MOD_ROWS = 8
MOD_N = 768
VMEM_LIMIT = 56 * 1024 * 1024

BF16 = jnp.bfloat16
F32 = jnp.float32


def _const_spec(shape):
    zeros = (0,) * len(shape)
    return pl.BlockSpec(shape, lambda *_: zeros, pipeline_mode=pl.Buffered(1))


def _params(sem):
    return pltpu.CompilerParams(dimension_semantics=sem, vmem_limit_bytes=VMEM_LIMIT)


def _mod_row_spec(layer):
    return pl.BlockSpec(
        (None, 1, 6 * D),
        lambda i: (layer * MOD_ROWS + jnp.minimum(i // TILES_PER_SEQ, BATCH), 0, 0))


def _rms_mod(x, g, shift, scale):
    y = x * lax.rsqrt(jnp.mean(x * x, axis=-1, keepdims=True) + EPS) * g
    return y * (1.0 + scale) + shift


def _dot(a, b):
    return jnp.dot(a, b, preferred_element_type=F32)


def _dot_nt(a, b):
    return lax.dot_general(a, b, (((1,), (1,)), ((), ())), preferred_element_type=F32)


def _mod_kernel(cc_ref, w_ref, b_ref, o_ref):
    cc = cc_ref[...]
    sc = (cc * jax.nn.sigmoid(cc)).astype(BF16)
    o_ref[...] = _dot(sc, w_ref[...].astype(BF16)) + b_ref[...]


def _modulation(cc, w_mod, b_mod):
    nblk = (6 * D) // MOD_N
    return pl.pallas_call(
        _mod_kernel,
        grid=(DEPTH, nblk),
        in_specs=[
            pl.BlockSpec((MOD_ROWS, D), lambda l, n: (0, 0)),
            pl.BlockSpec((None, D, MOD_N), lambda l, n: (l, 0, n)),
            pl.BlockSpec((None, 1, MOD_N), lambda l, n: (l, 0, n)),
        ],
        out_specs=pl.BlockSpec((None, MOD_ROWS, MOD_N), lambda l, n: (l, 0, n)),
        out_shape=jax.ShapeDtypeStruct((DEPTH, MOD_ROWS, 6 * D), F32),
        compiler_params=_params(("arbitrary", "arbitrary")),
    )(cc, w_mod, b_mod.reshape(DEPTH, 1, 6 * D))


def _even_in_kernel(x_ref, mod_ref, ng_ref, w_in_ref, lng_ref, w_sp_ref, b_sp_ref,
                    a_ref, q_ref, k_ref, v_ref):
    h = _rms_mod(x_ref[...], ng_ref[...], mod_ref[:, 0:D], mod_ref[:, D:2 * D]).astype(BF16)
    z = jax.nn.gelu(_dot(h, w_in_ref[:, 0:2 * A_W]))
    gd = A_W // A_GROUPS
    for grp in range(A_GROUPS):
        g = z[:, A_W + grp * gd:A_W + (grp + 1) * gd]
        mu = jnp.mean(g, axis=-1, keepdims=True)
        gc = g - mu
        var = jnp.mean(gc * gc, axis=-1, keepdims=True)
        gn = (gc * lax.rsqrt(var + EPS) * lng_ref[:, grp * gd:(grp + 1) * gd]).astype(BF16)
        for ch in range(TILE // CHUNK):
            rows = slice(ch * CHUNK, (ch + 1) * CHUNK)
            s = _dot(w_sp_ref[grp], gn[rows]) + b_sp_ref[grp]
            a_ref[rows, grp * gd:(grp + 1) * gd] = (z[rows, grp * gd:(grp + 1) * gd] * s).astype(BF16)
    c0 = 2 * A_W
    q_ref[...] = (_dot(h, w_in_ref[:, c0:c0 + B_W]) * (HEAD_DIM ** -0.5)).astype(BF16)
    k_ref[...] = _dot(h, w_in_ref[:, c0 + B_W:c0 + 2 * B_W]).astype(BF16)
    v_ref[...] = _dot(h, w_in_ref[:, c0 + 2 * B_W:c0 + 3 * B_W]).astype(BF16)


def _even_in(xs, mod3, layer, ng, w_in, lng, w_sp, b_sp):
    ntiles = T_ALL // TILE
    tok = lambda w: pl.BlockSpec((TILE, w), lambda i: (i, 0))
    out = jax.ShapeDtypeStruct((T_ALL, A_W), BF16)
    return pl.pallas_call(
        _even_in_kernel,
        grid=(ntiles,),
        in_specs=[
            tok(D), _mod_row_spec(layer), _const_spec((1, D)), _const_spec((D, MIX_IN)),
            _const_spec((1, A_W)), _const_spec((A_GROUPS, CHUNK, CHUNK)),
            _const_spec((A_GROUPS, CHUNK, 1)),
        ],
        out_specs=[tok(A_W)] * 4,
        out_shape=[out] * 4,
        compiler_params=_params(("arbitrary",)),
    )(xs, mod3, ng, w_in, lng, w_sp, b_sp)


def _attn_kernel(q_ref, k_ref, v_ref, kc_ref, vc_ref, tbl_ref, o_ref, *, ctx_queries):
    r = pl.program_id(1)
    lane = lax.broadcasted_iota(jnp.int32, (Q_BLK, 2 * HEAD_DIM), 1)
    first_head = lane < HEAD_DIM

    def attend(local):
        if local:
            start = pl.multiple_of(
                jnp.clip(r * Q_ROWS - WIN_H // 2, 0, ROWS - K_ROWS) * GRID_W, GRID_W)
        for hp in range(HEADS // 2):
            cols = slice(hp * 2 * HEAD_DIM, (hp + 1) * 2 * HEAD_DIM)
            q2 = q_ref[:, cols]
            kc2 = kc_ref[:, cols]
            vc2 = vc_ref[:, cols]
            if local:
                k2 = k_ref[pl.ds(start, K_BLK), cols]
                v2 = v_ref[pl.ds(start, K_BLK), cols]
            outs = []
            for half in range(2):
                qm = jnp.where(first_head if half == 0 else ~first_head, q2, jnp.zeros_like(q2))
                s_c = _dot_nt(qm, kc2)
                m = jnp.max(s_c, axis=-1, keepdims=True)
                if local:
                    s_l = _dot_nt(qm, k2) + tbl_ref[2 * hp + half]
                    m = jnp.maximum(m, jnp.max(s_l, axis=-1, keepdims=True))
                p_c = jnp.exp(s_c - m)
                den = jnp.sum(p_c, axis=-1, keepdims=True)
                o = _dot(p_c.astype(BF16), vc2)
                if local:
                    p_l = jnp.exp(s_l - m)
                    den = den + jnp.sum(p_l, axis=-1, keepdims=True)
                    o = o + _dot(p_l.astype(BF16), v2)
                outs.append(o / den)
            o_ref[:, cols] = jnp.where(first_head, outs[0], outs[1]).astype(BF16)

    if ctx_queries:
        pl.when(r < ROW_STEPS)(lambda: attend(True))
        pl.when(r >= ROW_STEPS)(lambda: attend(False))
    else:
        attend(True)


def _attention(q, k, v, tbl, ctx_queries):
    steps = ROW_STEPS + (1 if ctx_queries else 0)
    lat_blocks = T_LAT // Q_BLK

    def q_idx(b, r):
        return (jnp.where(r < ROW_STEPS, b * ROW_STEPS + r, lat_blocks + b), 0)

    def tbl_idx(b, r):
        return (jnp.where(r == 0, 0, jnp.where(r >= ROW_STEPS - 1, 2, 1)), 0, 0, 0)

    ctx_blk = T_LAT // CTX_LEN
    return pl.pallas_call(
        functools.partial(_attn_kernel, ctx_queries=ctx_queries),
        grid=(BATCH, steps),
        in_specs=[
            pl.BlockSpec((Q_BLK, B_W), q_idx),
            pl.BlockSpec((SEQ, B_W), lambda b, r: (b, 0)),
            pl.BlockSpec((SEQ, B_W), lambda b, r: (b, 0)),
            pl.BlockSpec((CTX_LEN, B_W), lambda b, r: (ctx_blk + b, 0)),
            pl.BlockSpec((CTX_LEN, B_W), lambda b, r: (ctx_blk + b, 0)),
            pl.BlockSpec((None, HEADS, Q_BLK, K_BLK), tbl_idx),
        ],
        out_specs=pl.BlockSpec((Q_BLK, B_W), q_idx),
        out_shape=jax.ShapeDtypeStruct((T_ALL if ctx_queries else T_LAT, B_W), BF16),
        compiler_params=_params(("arbitrary", "arbitrary")),
    )(q, k, v, k, v, tbl)


def _bias_tables(rpb):
    cols = np.arange(GRID_W)
    dc = np.clip(cols[None, :] - cols[:, None] + WIN_W - 1, 0, 2 * WIN_W - 2)
    cs = np.clip(cols - WIN_W // 2, 0, GRID_W - WIN_W)
    col_ok = (cols[None, :] >= cs[:, None]) & (cols[None, :] < cs[:, None] + WIN_W)
    tables = []
    for r0 in (0, Q_ROWS, ROWS - Q_ROWS):
        qrow = r0 + np.arange(Q_ROWS)
        krow = np.clip(r0 - WIN_H // 2, 0, ROWS - K_ROWS) + np.arange(K_ROWS)
        ws = np.clip(qrow - WIN_H // 2, 0, ROWS - WIN_H)
        row_ok = (krow[None, :] >= ws[:, None]) & (krow[None, :] < ws[:, None] + WIN_H)
        dr = np.clip(krow[None, :] - qrow[:, None] + WIN_H - 1, 0, 2 * WIN_H - 2)
        bias = rpb[:, dr[:, None, :, None], dc[None, :, None, :]]
        ok = row_ok[:, None, :, None] & col_ok[None, :, None, :]
        tables.append(jnp.where(ok[None], bias, NEG).reshape(HEADS, Q_BLK, K_BLK))
    return jnp.stack(tables)


def _ffn_tail(x, y, mod_ref, ng2_ref, w1_ref, w2_ref, fg_ref):
    x1 = x + mod_ref[:, 2 * D:3 * D] * y
    h2 = _rms_mod(x1, ng2_ref[...], mod_ref[:, 3 * D:4 * D], mod_ref[:, 4 * D:5 * D]).astype(BF16)
    acc = jnp.zeros_like(x1)
    for c in range(FFN // FFN_CHUNK):
        cs = slice(c * FFN_CHUNK, (c + 1) * FFN_CHUNK)
        hid = jnp.maximum(_dot(h2, w1_ref[:, cs]), 0.0)
        acc = acc + _dot((hid * hid).astype(BF16), w2_ref[cs, :])
    x2 = x1 + mod_ref[:, 5 * D:6 * D] * acc
    if fg_ref is not None:
        x2 = x2 * lax.rsqrt(jnp.mean(x2 * x2, axis=-1, keepdims=True) + EPS) * fg_ref[...]
    return x2


def _even_out_kernel(*refs, final):
    if final:
        a_ref, b_ref, x_ref, mod_ref, ng2_ref, wo_ref, w1_ref, w2_ref, fg_ref, o_ref = refs
    else:
        a_ref, b_ref, x_ref, mod_ref, ng2_ref, wo_ref, w1_ref, w2_ref, o_ref = refs
        fg_ref = None
    y = _dot(a_ref[...], wo_ref[0:A_W, :]) + _dot(b_ref[...], wo_ref[A_W:A_W + B_W, :])
    o_ref[...] = _ffn_tail(x_ref[...], y, mod_ref, ng2_ref, w1_ref, w2_ref, fg_ref)


def _even_out(a, b, xs, mod3, layer, ng2, w_out, w1, w2, n_out, final_g=None):
    tok = lambda w: pl.BlockSpec((TILE, w), lambda i: (i, 0))
    in_specs = [tok(A_W), tok(B_W), tok(D), _mod_row_spec(layer), _const_spec((1, D)),
                _const_spec((A_W + B_W, D)), _const_spec((D, FFN)), _const_spec((FFN, D))]
    args = [a, b, xs, mod3, ng2, w_out, w1, w2]
    if final_g is not None:
        in_specs.append(_const_spec((1, D)))
        args.append(final_g)
    return pl.pallas_call(
        functools.partial(_even_out_kernel, final=final_g is not None),
        grid=(n_out // TILE,),
        in_specs=in_specs,
        out_specs=tok(D),
        out_shape=jax.ShapeDtypeStruct((n_out, D), F32),
        compiler_params=_params(("arbitrary",)),
    )(*args)


def _odd_kernel(*refs, final):
    if final:
        (xp_ref, x_ref, xn_ref, mod_ref, ng1_ref, ng2_ref, wp1_ref, bp1_ref, wdw_ref, bdw_ref,
         lg_ref, lb_ref, wp2_ref, bp2_ref, w1_ref, w2_ref, fg_ref, o_ref, y_scr, conv_scr) = refs
    else:
        (xp_ref, x_ref, xn_ref, mod_ref, ng1_ref, ng2_ref, wp1_ref, bp1_ref, wdw_ref, bdw_ref,
         lg_ref, lb_ref, wp2_ref, bp2_ref, w1_ref, w2_ref, o_ref, y_scr, conv_scr) = refs
        fg_ref = None
    i = pl.program_id(0)
    is_lat = i < LAT_TILES
    pos = i % TILES_PER_SEQ
    seq_first = jnp.logical_or(jnp.logical_not(is_lat), pos == 0)
    seq_last = jnp.logical_or(jnp.logical_not(is_lat), pos == TILES_PER_SEQ - 1)

    x = x_ref[...]
    xe = jnp.concatenate([xp_ref[...], x, xn_ref[...]], axis=0)
    h = _rms_mod(xe, ng1_ref[...], mod_ref[:, 0:D], mod_ref[:, D:2 * D]).astype(BF16)
    t = _dot(h, wp1_ref[...]) + bp1_ref[...]
    glu = t[:, 0:D] * jax.nn.sigmoid(t[:, D:2 * D])
    row = lax.broadcasted_iota(jnp.int32, (HALO + TILE + HALO, 1), 0)
    pad = jnp.logical_or(jnp.logical_and(seq_first, row < HALO),
                         jnp.logical_and(seq_last, row >= HALO + TILE))
    y_scr[...] = jnp.where(pad, 0.0, glu)

    base = HALO - CONV_K // 2
    rows_in = CONV_ROWS + SUBLANES

    def conv_block(cb, carry):
        lanes = pl.ds(pl.multiple_of(cb * LANES, LANES), LANES)
        for r0 in range(0, TILE, CONV_ROWS):
            out = bdw_ref[:, lanes]
            for s in range(SUBLANES):
                part = None
                for off in range(s, base + CONV_K, SUBLANES):
                    if off < base:
                        continue
                    term = y_scr[pl.ds(r0 + off - s, rows_in), lanes] * wdw_ref[pl.ds(off - base, 1), lanes]
                    part = term if part is None else part + term
                out = out + part[s:s + CONV_ROWS]
            conv_scr[pl.ds(r0, CONV_ROWS), lanes] = out
        return carry

    lax.fori_loop(0, D // LANES, conv_block, 0)
    acc = conv_scr[...]
    mu = jnp.mean(acc, axis=-1, keepdims=True)
    ac = acc - mu
    var = jnp.mean(ac * ac, axis=-1, keepdims=True)
    ln = ac * lax.rsqrt(var + EPS) * lg_ref[...] + lb_ref[...]
    act = (ln * jax.nn.sigmoid(ln)).astype(BF16)
    y = _dot(act, wp2_ref[...]) + bp2_ref[...]
    o_ref[...] = _ffn_tail(x, y, mod_ref, ng2_ref, w1_ref, w2_ref, fg_ref)


def _odd_layer(xs, mod3, layer, ng1, ng2, wp1, bp1, wdw, bdw, lg, lb, wp2, bp2, w1, w2, final_g=None):
    n = xs.shape[0]
    per = TILE // HALO
    halo_blocks = n // HALO
    in_specs = [
        pl.BlockSpec((HALO, D), lambda i: (jnp.maximum(i * per - 1, 0), 0)),
        pl.BlockSpec((TILE, D), lambda i: (i, 0)),
        pl.BlockSpec((HALO, D), lambda i: (jnp.minimum((i + 1) * per, halo_blocks - 1), 0)),
        _mod_row_spec(layer), _const_spec((1, D)), _const_spec((1, D)),
        _const_spec((D, 2 * D)), _const_spec((1, 2 * D)), _const_spec((CONV_K, D)), _const_spec((1, D)),
        _const_spec((1, D)), _const_spec((1, D)), _const_spec((D, D)), _const_spec((1, D)),
        _const_spec((D, FFN)), _const_spec((FFN, D)),
    ]
    args = [xs, xs, xs, mod3, ng1, ng2, wp1, bp1, wdw, bdw, lg, lb, wp2, bp2, w1, w2]
    if final_g is not None:
        in_specs.append(_const_spec((1, D)))
        args.append(final_g)
    return pl.pallas_call(
        functools.partial(_odd_kernel, final=final_g is not None),
        grid=(n // TILE,),
        in_specs=in_specs,
        out_specs=pl.BlockSpec((TILE, D), lambda i: (i, 0)),
        out_shape=jax.ShapeDtypeStruct((n, D), F32),
        scratch_shapes=[pltpu.VMEM((HALO + TILE + HALO, D), F32), pltpu.VMEM((TILE, D), F32)],
        compiler_params=_params(("arbitrary",)),
    )(*args)


def kernel(x, c, ctx, c_ctx, w_mod, b_mod, norm_g, w_in, w_out, ln_v_g, w_sp, b_sp, rpb, w_pw1, b_pw1,
           w_dw, b_dw, ln_c_g, ln_c_b, w_pw2, b_pw2, w_ff1, w_ff2, final_g):
    row = lambda t: t.reshape(1, -1)
    cc = jnp.concatenate([c, c_ctx[None, :], jnp.zeros((MOD_ROWS - BATCH - 1, D), F32)], axis=0)
    mod3 = _modulation(cc, w_mod, b_mod).reshape(DEPTH * MOD_ROWS, 1, 6 * D)
    xs = jnp.concatenate([x.reshape(T_LAT, D), ctx.reshape(T_CTX, D)], axis=0)
    last_reader = ((DEPTH - 1) // 2) * 2
    for l in range(DEPTH):
        ctx_out = l < last_reader
        fg = row(final_g) if l == DEPTH - 1 else None
        w1 = w_ff1[l].astype(BF16)
        w2 = w_ff2[l].astype(BF16)
        j = l // 2
        if l % 2 == 0:
            a, q, k, v = _even_in(xs, mod3, l, row(norm_g[l, 0]), w_in[j].astype(BF16), row(ln_v_g[j]),
                                  w_sp[j].astype(BF16), b_sp[j][:, :, None])
            b = _attention(q, k, v, _bias_tables(rpb[j]), ctx_out)
            xs = _even_out(a, b, xs, mod3, l, row(norm_g[l, 1]), w_out[j].astype(BF16), w1, w2,
                           T_ALL if ctx_out else T_LAT, fg)
        else:
            if not ctx_out and xs.shape[0] != T_LAT:
                xs = xs[:T_LAT]
            xs = _odd_layer(xs, mod3, l, row(norm_g[l, 0]), row(norm_g[l, 1]), w_pw1[j].astype(BF16),
                            row(b_pw1[j]), w_dw[j], row(b_dw[j]), row(ln_c_g[j]), row(ln_c_b[j]),
                            w_pw2[j].astype(BF16), row(b_pw2[j]), w1, w2, fg)
    return xs[:T_LAT].reshape(BATCH, SEQ, D)
```

```python
import functools

import jax
import jax.numpy as jnp
from jax import lax
from jax.experimental import pallas as pl
from jax.experimental.pallas import tpu as pltpu

D = 1024
BATCH = 4
SEQ = 4096
DEPTH = 4
GRID_W = 64
ROWS = SEQ // GRID_W
CTX_LEN = 256
CHUNK = 128
A_GROUPS = 4
A_W = 512
B_W = 512
HEAD_DIM = 64
HEADS = 8
WIN_H = 8
WIN_W = 16
Q_ROWS = 4
Q_BLK = Q_ROWS * GRID_W
K_ROWS = Q_ROWS + WIN_H
K_BLK = K_ROWS * GRID_W
ROW_STEPS = ROWS // Q_ROWS
MIX_IN = 2 * A_W + 3 * B_W
CONV_K = 31
HALO = 16
SUBLANES = 8
LANES = 128
CONV_ROWS = 128
FFN = 4 * D
FFN_CHUNK = 1024
EPS = 1e-6
NEG = -1e30

T_LAT = BATCH * SEQ
T_CTX = BATCH * CTX_LEN
T_ALL = T_LAT + T_CTX
TILE = 256
TILES_PER_SEQ = SEQ // TILE
LAT_TILES = T_LAT // TILE
MOD_ROWS = 8
MOD_N = 768
VMEM_LIMIT = 56 * 1024 * 1024

BF16 = jnp.bfloat16
F32 = jnp.float32


def _const_spec(shape):
    zeros = (0,) * len(shape)
    return pl.BlockSpec(shape, lambda *_: zeros, pipeline_mode=pl.Buffered(1))


def _params(sem):
    return pltpu.CompilerParams(dimension_semantics=sem, vmem_limit_bytes=VMEM_LIMIT)


def _mod_row_spec(layer):
    return pl.BlockSpec(
        (None, 1, 6 * D),
        lambda i: (layer * MOD_ROWS + jnp.minimum(i // TILES_PER_SEQ, BATCH), 0, 0))


def _rms_mod(x, g, shift, scale):
    y = x * lax.rsqrt(jnp.mean(x * x, axis=-1, keepdims=True) + EPS) * g
    return y * (1.0 + scale) + shift


def _dot(a, b):
    return jnp.dot(a, b, preferred_element_type=F32)


def _dot_nt(a, b):
    return lax.dot_general(a, b, (((1,), (1,)), ((), ())), preferred_element_type=F32)


def _mod_kernel(cc_ref, w_ref, b_ref, o_ref):
    cc = cc_ref[...]
    sc = (cc * jax.nn.sigmoid(cc)).astype(BF16)
    o_ref[...] = _dot(sc, w_ref[...].astype(BF16)) + b_ref[...]


def _modulation(cc, w_mod, b_mod):
    nblk = (6 * D) // MOD_N
    return pl.pallas_call(
        _mod_kernel,
        grid=(DEPTH, nblk),
        in_specs=[
            pl.BlockSpec((MOD_ROWS, D), lambda l, n: (0, 0)),
            pl.BlockSpec((None, D, MOD_N), lambda l, n: (l, 0, n)),
            pl.BlockSpec((None, 1, MOD_N), lambda l, n: (l, 0, n)),
        ],
        out_specs=pl.BlockSpec((None, MOD_ROWS, MOD_N), lambda l, n: (l, 0, n)),
        out_shape=jax.ShapeDtypeStruct((DEPTH, MOD_ROWS, 6 * D), F32),
        compiler_params=_params(("arbitrary", "arbitrary")),
    )(cc, w_mod, b_mod.reshape(DEPTH, 1, 6 * D))


def _even_in_kernel(x_ref, mod_ref, ng_ref, w_in_ref, lng_ref, w_sp_ref, b_sp_ref,
                    a_ref, q_ref, k_ref, v_ref):
    h = _rms_mod(x_ref[...], ng_ref[...], mod_ref[:, 0:D], mod_ref[:, D:2 * D]).astype(BF16)
    z = jax.nn.gelu(_dot(h, w_in_ref[:, 0:2 * A_W]))
    gd = A_W // A_GROUPS
    for grp in range(A_GROUPS):
        g = z[:, A_W + grp * gd:A_W + (grp + 1) * gd]
        mu = jnp.mean(g, axis=-1, keepdims=True)
        gc = g - mu
        var = jnp.mean(gc * gc, axis=-1, keepdims=True)
        gn = (gc * lax.rsqrt(var + EPS) * lng_ref[:, grp * gd:(grp + 1) * gd]).astype(BF16)
        for ch in range(TILE // CHUNK):
            rows = slice(ch * CHUNK, (ch + 1) * CHUNK)
            s = _dot(w_sp_ref[grp], gn[rows]) + b_sp_ref[grp]
            a_ref[rows, grp * gd:(grp + 1) * gd] = (z[rows, grp * gd:(grp + 1) * gd] * s).astype(BF16)
    c0 = 2 * A_W
    q_ref[...] = (_dot(h, w_in_ref[:, c0:c0 + B_W]) * (HEAD_DIM ** -0.5)).astype(BF16)
    k_ref[...] = _dot(h, w_in_ref[:, c0 + B_W:c0 + 2 * B_W]).astype(BF16)
    v_ref[...] = _dot(h, w_in_ref[:, c0 + 2 * B_W:c0 + 3 * B_W]).astype(BF16)


def _even_in(xs, mod3, layer, ng, w_in, lng, w_sp, b_sp):
    ntiles = T_ALL // TILE
    tok = lambda w: pl.BlockSpec((TILE, w), lambda i: (i, 0))
    out = jax.ShapeDtypeStruct((T_ALL, A_W), BF16)
    return pl.pallas_call(
        _even_in_kernel,
        grid=(ntiles,),
        in_specs=[
            tok(D), _mod_row_spec(layer), _const_spec((1, D)), _const_spec((D, MIX_IN)),
            _const_spec((1, A_W)), _const_spec((A_GROUPS, CHUNK, CHUNK)),
            _const_spec((A_GROUPS, CHUNK, 1)),
        ],
        out_specs=[tok(A_W)] * 4,
        out_shape=[out] * 4,
        compiler_params=_params(("arbitrary",)),
    )(xs, mod3, ng, w_in, lng, w_sp, b_sp)


def _attn_kernel(q_ref, k_ref, v_ref, kc_ref, vc_ref, tbl_ref, o_ref, *, ctx_queries):
    r = pl.program_id(1)
    lane = lax.broadcasted_iota(jnp.int32, (Q_BLK, 2 * HEAD_DIM), 1)
    first_head = lane < HEAD_DIM

    def attend(local):
        if local:
            start = pl.multiple_of(
                jnp.clip(r * Q_ROWS - WIN_H // 2, 0, ROWS - K_ROWS) * GRID_W, GRID_W)
        for hp in range(HEADS // 2):
            cols = slice(hp * 2 * HEAD_DIM, (hp + 1) * 2 * HEAD_DIM)
            q2 = q_ref[:, cols]
            kc2 = kc_ref[:, cols]
            vc2 = vc_ref[:, cols]
            if local:
                k2 = k_ref[pl.ds(start, K_BLK), cols]
                v2 = v_ref[pl.ds(start, K_BLK), cols]
            outs = []
            for half in range(2):
                qm = jnp.where(first_head if half == 0 else ~first_head, q2, jnp.zeros_like(q2))
                s_c = _dot_nt(qm, kc2)
                m = jnp.max(s_c, axis=-1, keepdims=True)
                if local:
                    s_l = _dot_nt(qm, k2) + tbl_ref[2 * hp + half]
                    m = jnp.maximum(m, jnp.max(s_l, axis=-1, keepdims=True))
                p_c = jnp.exp(s_c - m)
                den = jnp.sum(p_c, axis=-1, keepdims=True)
                o = _dot(p_c.astype(BF16), vc2)
                if local:
                    p_l = jnp.exp(s_l - m)
                    den = den + jnp.sum(p_l, axis=-1, keepdims=True)
                    o = o + _dot(p_l.astype(BF16), v2)
                outs.append(o / den)
            o_ref[:, cols] = jnp.where(first_head, outs[0], outs[1]).astype(BF16)

    if ctx_queries:
        pl.when(r < ROW_STEPS)(lambda: attend(True))
        pl.when(r >= ROW_STEPS)(lambda: attend(False))
    else:
        attend(True)


def _attention(q, k, v, tbl, ctx_queries):
    steps = ROW_STEPS + (1 if ctx_queries else 0)
    lat_blocks = T_LAT // Q_BLK

    def q_idx(b, r):
        return (jnp.where(r < ROW_STEPS, b * ROW_STEPS + r, lat_blocks + b), 0)

    def tbl_idx(b, r):
        return (jnp.where(r == 0, 0, jnp.where(r >= ROW_STEPS - 1, 2, 1)), 0, 0, 0)

    ctx_blk = T_LAT // CTX_LEN
    return pl.pallas_call(
        functools.partial(_attn_kernel, ctx_queries=ctx_queries),
        grid=(BATCH, steps),
        in_specs=[
            pl.BlockSpec((Q_BLK, B_W), q_idx),
            pl.BlockSpec((SEQ, B_W), lambda b, r: (b, 0)),
            pl.BlockSpec((SEQ, B_W), lambda b, r: (b, 0)),
            pl.BlockSpec((CTX_LEN, B_W), lambda b, r: (ctx_blk + b, 0)),
            pl.BlockSpec((CTX_LEN, B_W), lambda b, r: (ctx_blk + b, 0)),
            pl.BlockSpec((None, HEADS, Q_BLK, K_BLK), tbl_idx),
        ],
        out_specs=pl.BlockSpec((Q_BLK, B_W), q_idx),
        out_shape=jax.ShapeDtypeStruct((T_ALL if ctx_queries else T_LAT, B_W), BF16),
        compiler_params=_params(("arbitrary", "arbitrary")),
    )(q, k, v, k, v, tbl)


N_DR = 2 * WIN_H - 1
N_DC = 2 * WIN_W - 1
TBL_R0 = (0, Q_ROWS, ROWS - Q_ROWS)


def _tbl_kernel(rpb_ref, o_ref, t_scr):
    h = pl.program_id(0)
    shape = (GRID_W, 2 * GRID_W)
    cq = lax.broadcasted_iota(jnp.int32, shape, 0)
    lane = lax.broadcasted_iota(jnp.int32, shape, 1)
    ck = lane & (GRID_W - 1)
    dc = jnp.clip(ck - cq + WIN_W - 1, 0, N_DC - 1)
    cs = jnp.clip(cq - WIN_W // 2, 0, GRID_W - WIN_W)
    col_ok = (ck >= cs) & (ck < cs + WIN_W)
    neg = jnp.full(shape, NEG, F32)
    for dr in range(N_DR):
        t = neg
        for j in range(N_DC):
            t = jnp.where(dc == j, rpb_ref[(h * N_DR + dr) * N_DC + j], t)
        t_scr[dr] = jnp.where(col_ok, t, NEG)
    left = lane < GRID_W
    for v, r0 in enumerate(TBL_R0):
        k0 = min(max(r0 - WIN_H // 2, 0), ROWS - K_ROWS)
        for qi in range(Q_ROWS):
            qrow = r0 + qi
            w0 = min(max(qrow - WIN_H // 2, 0), ROWS - WIN_H)
            for kp in range(K_ROWS // 2):
                halves = []
                for krow in (k0 + 2 * kp, k0 + 2 * kp + 1):
                    in_win = w0 <= krow < w0 + WIN_H
                    halves.append(t_scr[krow - qrow + WIN_H - 1] if in_win else neg)
                o_ref[v, qi * GRID_W:(qi + 1) * GRID_W, kp * 2 * GRID_W:(kp + 1) * 2 * GRID_W] = (
                    jnp.where(left, halves[0], halves[1]))


def _bias_tables(rpb):
    return pl.pallas_call(
        _tbl_kernel,
        grid=(HEADS,),
        in_specs=[pl.BlockSpec(memory_space=pltpu.SMEM)],
        out_specs=pl.BlockSpec((len(TBL_R0), None, Q_BLK, K_BLK), lambda h: (0, h, 0, 0)),
        out_shape=jax.ShapeDtypeStruct((len(TBL_R0), HEADS, Q_BLK, K_BLK), F32),
        scratch_shapes=[pltpu.VMEM((N_DR, GRID_W, 2 * GRID_W), F32)],
        compiler_params=_params(("arbitrary",)),
    )(rpb.reshape(-1))


def _ffn_tail(x, y, mod_ref, ng2_ref, w1_ref, w2_ref, fg_ref):
    x1 = x + mod_ref[:, 2 * D:3 * D] * y
    h2 = _rms_mod(x1, ng2_ref[...], mod_ref[:, 3 * D:4 * D], mod_ref[:, 4 * D:5 * D]).astype(BF16)
    acc = jnp.zeros_like(x1)
    for c in range(FFN // FFN_CHUNK):
        cs = slice(c * FFN_CHUNK, (c + 1) * FFN_CHUNK)
        hid = jnp.maximum(_dot(h2, w1_ref[:, cs]), 0.0)
        acc = acc + _dot((hid * hid).astype(BF16), w2_ref[cs, :])
    x2 = x1 + mod_ref[:, 5 * D:6 * D] * acc
    if fg_ref is not None:
        x2 = x2 * lax.rsqrt(jnp.mean(x2 * x2, axis=-1, keepdims=True) + EPS) * fg_ref[...]
    return x2


def _even_out_kernel(*refs, final):
    if final:
        a_ref, b_ref, x_ref, mod_ref, ng2_ref, wo_ref, w1_ref, w2_ref, fg_ref, o_ref = refs
    else:
        a_ref, b_ref, x_ref, mod_ref, ng2_ref, wo_ref, w1_ref, w2_ref, o_ref = refs
        fg_ref = None
    y = _dot(a_ref[...], wo_ref[0:A_W, :]) + _dot(b_ref[...], wo_ref[A_W:A_W + B_W, :])
    o_ref[...] = _ffn_tail(x_ref[...], y, mod_ref, ng2_ref, w1_ref, w2_ref, fg_ref)


def _even_out(a, b, xs, mod3, layer, ng2, w_out, w1, w2, n_out, final_g=None):
    tok = lambda w: pl.BlockSpec((TILE, w), lambda i: (i, 0))
    in_specs = [tok(A_W), tok(B_W), tok(D), _mod_row_spec(layer), _const_spec((1, D)),
                _const_spec((A_W + B_W, D)), _const_spec((D, FFN)), _const_spec((FFN, D))]
    args = [a, b, xs, mod3, ng2, w_out, w1, w2]
    if final_g is not None:
        in_specs.append(_const_spec((1, D)))
        args.append(final_g)
    return pl.pallas_call(
        functools.partial(_even_out_kernel, final=final_g is not None),
        grid=(n_out // TILE,),
        in_specs=in_specs,
        out_specs=tok(D),
        out_shape=jax.ShapeDtypeStruct((n_out, D), F32),
        compiler_params=_params(("arbitrary",)),
    )(*args)


def _odd_kernel(*refs, final):
    if final:
        (xp_ref, x_ref, xn_ref, mod_ref, ng1_ref, ng2_ref, wp1_ref, bp1_ref, wdw_ref, bdw_ref,
         lg_ref, lb_ref, wp2_ref, bp2_ref, w1_ref, w2_ref, fg_ref, o_ref, y_scr, conv_scr) = refs
    else:
        (xp_ref, x_ref, xn_ref, mod_ref, ng1_ref, ng2_ref, wp1_ref, bp1_ref, wdw_ref, bdw_ref,
         lg_ref, lb_ref, wp2_ref, bp2_ref, w1_ref, w2_ref, o_ref, y_scr, conv_scr) = refs
        fg_ref = None
    i = pl.program_id(0)
    is_lat = i < LAT_TILES
    pos = i % TILES_PER_SEQ
    seq_first = jnp.logical_or(jnp.logical_not(is_lat), pos == 0)
    seq_last = jnp.logical_or(jnp.logical_not(is_lat), pos == TILES_PER_SEQ - 1)

    x = x_ref[...]
    xe = jnp.concatenate([xp_ref[...], x, xn_ref[...]], axis=0)
    h = _rms_mod(xe, ng1_ref[...], mod_ref[:, 0:D], mod_ref[:, D:2 * D]).astype(BF16)
    t = _dot(h, wp1_ref[...]) + bp1_ref[...]
    glu = t[:, 0:D] * jax.nn.sigmoid(t[:, D:2 * D])
    row = lax.broadcasted_iota(jnp.int32, (HALO + TILE + HALO, 1), 0)
    pad = jnp.logical_or(jnp.logical_and(seq_first, row < HALO),
                         jnp.logical_and(seq_last, row >= HALO + TILE))
    y_scr[...] = jnp.where(pad, 0.0, glu)

    base = HALO - CONV_K // 2
    rows_in = CONV_ROWS + SUBLANES

    def conv_block(cb, carry):
        lanes = pl.ds(pl.multiple_of(cb * LANES, LANES), LANES)
        for r0 in range(0, TILE, CONV_ROWS):
            out = bdw_ref[:, lanes]
            for s in range(SUBLANES):
                part = None
                for off in range(s, base + CONV_K, SUBLANES):
                    if off < base:
                        continue
                    term = y_scr[pl.ds(r0 + off - s, rows_in), lanes] * wdw_ref[pl.ds(off - base, 1), lanes]
                    part = term if part is None else part + term
                out = out + part[s:s + CONV_ROWS]
            conv_scr[pl.ds(r0, CONV_ROWS), lanes] = out
        return carry

    lax.fori_loop(0, D // LANES, conv_block, 0)
    acc = conv_scr[...]
    mu = jnp.mean(acc, axis=-1, keepdims=True)
    ac = acc - mu
    var = jnp.mean(ac * ac, axis=-1, keepdims=True)
    ln = ac * lax.rsqrt(var + EPS) * lg_ref[...] + lb_ref[...]
    act = (ln * jax.nn.sigmoid(ln)).astype(BF16)
    y = _dot(act, wp2_ref[...]) + bp2_ref[...]
    o_ref[...] = _ffn_tail(x, y, mod_ref, ng2_ref, w1_ref, w2_ref, fg_ref)


def _odd_layer(xs, mod3, layer, ng1, ng2, wp1, bp1, wdw, bdw, lg, lb, wp2, bp2, w1, w2, final_g=None):
    n = xs.shape[0]
    per = TILE // HALO
    halo_blocks = n // HALO
    in_specs = [
        pl.BlockSpec((HALO, D), lambda i: (jnp.maximum(i * per - 1, 0), 0)),
        pl.BlockSpec((TILE, D), lambda i: (i, 0)),
        pl.BlockSpec((HALO, D), lambda i: (jnp.minimum((i + 1) * per, halo_blocks - 1), 0)),
        _mod_row_spec(layer), _const_spec((1, D)), _const_spec((1, D)),
        _const_spec((D, 2 * D)), _const_spec((1, 2 * D)), _const_spec((CONV_K, D)), _const_spec((1, D)),
        _const_spec((1, D)), _const_spec((1, D)), _const_spec((D, D)), _const_spec((1, D)),
        _const_spec((D, FFN)), _const_spec((FFN, D)),
    ]
    args = [xs, xs, xs, mod3, ng1, ng2, wp1, bp1, wdw, bdw, lg, lb, wp2, bp2, w1, w2]
    if final_g is not None:
        in_specs.append(_const_spec((1, D)))
        args.append(final_g)
    return pl.pallas_call(
        functools.partial(_odd_kernel, final=final_g is not None),
        grid=(n // TILE,),
        in_specs=in_specs,
        out_specs=pl.BlockSpec((TILE, D), lambda i: (i, 0)),
        out_shape=jax.ShapeDtypeStruct((n, D), F32),
        scratch_shapes=[pltpu.VMEM((HALO + TILE + HALO, D), F32), pltpu.VMEM((TILE, D), F32)],
        compiler_params=_params(("arbitrary",)),
    )(*args)


def kernel(x, c, ctx, c_ctx, w_mod, b_mod, norm_g, w_in, w_out, ln_v_g, w_sp, b_sp, rpb, w_pw1, b_pw1,
           w_dw, b_dw, ln_c_g, ln_c_b, w_pw2, b_pw2, w_ff1, w_ff2, final_g):
    row = lambda t: t.reshape(1, -1)
    cc = jnp.concatenate([c, c_ctx[None, :], jnp.zeros((MOD_ROWS - BATCH - 1, D), F32)], axis=0)
    mod3 = _modulation(cc, w_mod, b_mod).reshape(DEPTH * MOD_ROWS, 1, 6 * D)
    xs = jnp.concatenate([x.reshape(T_LAT, D), ctx.reshape(T_CTX, D)], axis=0)
    last_reader = ((DEPTH - 1) // 2) * 2
    for l in range(DEPTH):
        ctx_out = l < last_reader
        fg = row(final_g) if l == DEPTH - 1 else None
        w1 = w_ff1[l].astype(BF16)
        w2 = w_ff2[l].astype(BF16)
        j = l // 2
        if l % 2 == 0:
            a, q, k, v = _even_in(xs, mod3, l, row(norm_g[l, 0]), w_in[j].astype(BF16), row(ln_v_g[j]),
                                  w_sp[j].astype(BF16), b_sp[j][:, :, None])
            b = _attention(q, k, v, _bias_tables(rpb[j]), ctx_out)
            xs = _even_out(a, b, xs, mod3, l, row(norm_g[l, 1]), w_out[j].astype(BF16), w1, w2,
                           T_ALL if ctx_out else T_LAT, fg)
        else:
            if not ctx_out and xs.shape[0] != T_LAT:
                xs = xs[:T_LAT]
            xs = _odd_layer(xs, mod3, l, row(norm_g[l, 0]), row(norm_g[l, 1]), w_pw1[j].astype(BF16),
                            row(b_pw1[j]), w_dw[j], row(b_dw[j]), row(ln_c_g[j]), row(ln_c_b[j]),
                            w_pw2[j].astype(BF16), row(b_pw2[j]), w1, w2, fg)
    return xs[:T_LAT].reshape(BATCH, SEQ, D)
```

```python
import functools

import jax
import jax.numpy as jnp
from jax import lax
from jax.experimental import pallas as pl
from jax.experimental.pallas import tpu as pltpu

D = 1024
BATCH = 4
SEQ = 4096
DEPTH = 4
GRID_W = 64
ROWS = SEQ // GRID_W
CTX_LEN = 256
CHUNK = 128
A_GROUPS = 4
A_W = 512
B_W = 512
HEAD_DIM = 64
HEADS = 8
WIN_H = 8
WIN_W = 16
Q_ROWS = 4
Q_BLK = Q_ROWS * GRID_W
K_ROWS = Q_ROWS + WIN_H
K_BLK = K_ROWS * GRID_W
ROW_STEPS = ROWS // Q_ROWS
MIX_IN = 2 * A_W + 3 * B_W
CONV_K = 31
HALO = 16
SUBLANES = 8
LANES = 128
CONV_ROWS = 128
FFN = 4 * D
FFN_CHUNK = 1024
FFN_CHUNKS = FFN // FFN_CHUNK
EPS = 1e-6
NEG = -1e30

T_LAT = BATCH * SEQ
T_CTX = BATCH * CTX_LEN
T_ALL = T_LAT + T_CTX
TILE = 256
CONV_PIECES = (D // LANES) * (TILE // CONV_ROWS)
TILES_PER_SEQ = SEQ // TILE
LAT_TILES = T_LAT // TILE
MOD_ROWS = 8
MOD_N = 768
VMEM_LIMIT = 56 * 1024 * 1024

BF16 = jnp.bfloat16
F32 = jnp.float32


def _const_spec(shape):
    zeros = (0,) * len(shape)
    return pl.BlockSpec(shape, lambda *_: zeros, pipeline_mode=pl.Buffered(1))


def _params(sem):
    return pltpu.CompilerParams(dimension_semantics=sem, vmem_limit_bytes=VMEM_LIMIT)


def _mod_row_spec(layer, tile_of=lambda i: i):
    return pl.BlockSpec(
        (None, 1, 6 * D),
        lambda i: (layer * MOD_ROWS + jnp.minimum(tile_of(i) // TILES_PER_SEQ, BATCH), 0, 0))


def _rms_mod(x, g, shift, scale):
    y = x * lax.rsqrt(jnp.mean(x * x, axis=-1, keepdims=True) + EPS) * g
    return y * (1.0 + scale) + shift


def _dot(a, b):
    return jnp.dot(a, b, preferred_element_type=F32)


def _dot_nt(a, b):
    return lax.dot_general(a, b, (((1,), (1,)), ((), ())), preferred_element_type=F32)


def _mod_kernel(cc_ref, w_ref, b_ref, o_ref):
    cc = cc_ref[...]
    sc = (cc * jax.nn.sigmoid(cc)).astype(BF16)
    o_ref[...] = _dot(sc, w_ref[...].astype(BF16)) + b_ref[...]


def _modulation(cc, w_mod, b_mod):
    nblk = (6 * D) // MOD_N
    return pl.pallas_call(
        _mod_kernel,
        grid=(DEPTH, nblk),
        in_specs=[
            pl.BlockSpec((MOD_ROWS, D), lambda l, n: (0, 0)),
            pl.BlockSpec((None, D, MOD_N), lambda l, n: (l, 0, n)),
            pl.BlockSpec((None, 1, MOD_N), lambda l, n: (l, 0, n)),
        ],
        out_specs=pl.BlockSpec((None, MOD_ROWS, MOD_N), lambda l, n: (l, 0, n)),
        out_shape=jax.ShapeDtypeStruct((DEPTH, MOD_ROWS, 6 * D), F32),
        compiler_params=_params(("arbitrary", "arbitrary")),
    )(cc, w_mod, b_mod.reshape(DEPTH, 1, 6 * D))


def _even_in_kernel(x_ref, mod_ref, ng_ref, w_in_ref, lng_ref, w_sp_ref, b_sp_ref,
                    a_ref, q_ref, k_ref, v_ref):
    h = _rms_mod(x_ref[...], ng_ref[...], mod_ref[:, 0:D], mod_ref[:, D:2 * D]).astype(BF16)
    z = jax.nn.gelu(_dot(h, w_in_ref[:, 0:2 * A_W]))
    gd = A_W // A_GROUPS
    for grp in range(A_GROUPS):
        g = z[:, A_W + grp * gd:A_W + (grp + 1) * gd]
        mu = jnp.mean(g, axis=-1, keepdims=True)
        gc = g - mu
        var = jnp.mean(gc * gc, axis=-1, keepdims=True)
        gn = (gc * lax.rsqrt(var + EPS) * lng_ref[:, grp * gd:(grp + 1) * gd]).astype(BF16)
        for ch in range(TILE // CHUNK):
            rows = slice(ch * CHUNK, (ch + 1) * CHUNK)
            s = _dot(w_sp_ref[grp], gn[rows]) + b_sp_ref[grp]
            a_ref[rows, grp * gd:(grp + 1) * gd] = (z[rows, grp * gd:(grp + 1) * gd] * s).astype(BF16)
    c0 = 2 * A_W
    q_ref[...] = (_dot(h, w_in_ref[:, c0:c0 + B_W]) * (HEAD_DIM ** -0.5)).astype(BF16)
    k_ref[...] = _dot(h, w_in_ref[:, c0 + B_W:c0 + 2 * B_W]).astype(BF16)
    v_ref[...] = _dot(h, w_in_ref[:, c0 + 2 * B_W:c0 + 3 * B_W]).astype(BF16)


def _even_in(xs, mod3, layer, ng, w_in, lng, w_sp, b_sp):
    ntiles = T_ALL // TILE
    tok = lambda w: pl.BlockSpec((TILE, w), lambda i: (i, 0))
    out = jax.ShapeDtypeStruct((T_ALL, A_W), BF16)
    return pl.pallas_call(
        _even_in_kernel,
        grid=(ntiles,),
        in_specs=[
            tok(D), _mod_row_spec(layer), _const_spec((1, D)), _const_spec((D, MIX_IN)),
            _const_spec((1, A_W)), _const_spec((A_GROUPS, CHUNK, CHUNK)),
            _const_spec((A_GROUPS, CHUNK, 1)),
        ],
        out_specs=[tok(A_W)] * 4,
        out_shape=[out] * 4,
        compiler_params=_params(("arbitrary",)),
    )(xs, mod3, ng, w_in, lng, w_sp, b_sp)


def _attn_kernel(q_ref, k_ref, v_ref, kc_ref, vc_ref, tbl_ref, o_ref, *, ctx_queries):
    r = pl.program_id(1)
    lane = lax.broadcasted_iota(jnp.int32, (Q_BLK, 2 * HEAD_DIM), 1)
    first_head = lane < HEAD_DIM

    def attend(local):
        if local:
            start = pl.multiple_of(
                jnp.clip(r * Q_ROWS - WIN_H // 2, 0, ROWS - K_ROWS) * GRID_W, GRID_W)
        for hp in range(HEADS // 2):
            cols = slice(hp * 2 * HEAD_DIM, (hp + 1) * 2 * HEAD_DIM)
            q2 = q_ref[:, cols]
            kc2 = kc_ref[:, cols]
            vc2 = vc_ref[:, cols]
            if local:
                k2 = k_ref[pl.ds(start, K_BLK), cols]
                v2 = v_ref[pl.ds(start, K_BLK), cols]
            outs = []
            for half in range(2):
                qm = jnp.where(first_head if half == 0 else ~first_head, q2, jnp.zeros_like(q2))
                s_c = _dot_nt(qm, kc2)
                m = jnp.max(s_c, axis=-1, keepdims=True)
                if local:
                    s_l = _dot_nt(qm, k2) + tbl_ref[2 * hp + half]
                    m = jnp.maximum(m, jnp.max(s_l, axis=-1, keepdims=True))
                p_c = jnp.exp(s_c - m)
                den = jnp.sum(p_c, axis=-1, keepdims=True)
                o = _dot(p_c.astype(BF16), vc2)
                if local:
                    p_l = jnp.exp(s_l - m)
                    den = den + jnp.sum(p_l, axis=-1, keepdims=True)
                    o = o + _dot(p_l.astype(BF16), v2)
                outs.append(o / den)
            o_ref[:, cols] = jnp.where(first_head, outs[0], outs[1]).astype(BF16)

    if ctx_queries:
        pl.when(r < ROW_STEPS)(lambda: attend(True))
        pl.when(r >= ROW_STEPS)(lambda: attend(False))
    else:
        attend(True)


def _attention(q, k, v, tbl, ctx_queries):
    steps = ROW_STEPS + (1 if ctx_queries else 0)
    lat_blocks = T_LAT // Q_BLK

    def q_idx(b, r):
        return (jnp.where(r < ROW_STEPS, b * ROW_STEPS + r, lat_blocks + b), 0)

    def tbl_idx(b, r):
        return (jnp.where(r == 0, 0, jnp.where(r >= ROW_STEPS - 1, 2, 1)), 0, 0, 0)

    ctx_blk = T_LAT // CTX_LEN
    return pl.pallas_call(
        functools.partial(_attn_kernel, ctx_queries=ctx_queries),
        grid=(BATCH, steps),
        in_specs=[
            pl.BlockSpec((Q_BLK, B_W), q_idx),
            pl.BlockSpec((SEQ, B_W), lambda b, r: (b, 0)),
            pl.BlockSpec((SEQ, B_W), lambda b, r: (b, 0)),
            pl.BlockSpec((CTX_LEN, B_W), lambda b, r: (ctx_blk + b, 0)),
            pl.BlockSpec((CTX_LEN, B_W), lambda b, r: (ctx_blk + b, 0)),
            pl.BlockSpec((None, HEADS, Q_BLK, K_BLK), tbl_idx),
        ],
        out_specs=pl.BlockSpec((Q_BLK, B_W), q_idx),
        out_shape=jax.ShapeDtypeStruct((T_ALL if ctx_queries else T_LAT, B_W), BF16),
        compiler_params=_params(("arbitrary", "arbitrary")),
    )(q, k, v, k, v, tbl)


N_DR = 2 * WIN_H - 1
N_DC = 2 * WIN_W - 1
TBL_R0 = (0, Q_ROWS, ROWS - Q_ROWS)


def _tbl_kernel(rpb_ref, o_ref, t_scr):
    h = pl.program_id(0)
    shape = (GRID_W, 2 * GRID_W)
    cq = lax.broadcasted_iota(jnp.int32, shape, 0)
    lane = lax.broadcasted_iota(jnp.int32, shape, 1)
    ck = lane & (GRID_W - 1)
    dc = jnp.clip(ck - cq + WIN_W - 1, 0, N_DC - 1)
    cs = jnp.clip(cq - WIN_W // 2, 0, GRID_W - WIN_W)
    col_ok = (ck >= cs) & (ck < cs + WIN_W)
    neg = jnp.full(shape, NEG, F32)
    for dr in range(N_DR):
        t = neg
        for j in range(N_DC):
            t = jnp.where(dc == j, rpb_ref[(h * N_DR + dr) * N_DC + j], t)
        t_scr[dr] = jnp.where(col_ok, t, NEG)
    left = lane < GRID_W
    for v, r0 in enumerate(TBL_R0):
        k0 = min(max(r0 - WIN_H // 2, 0), ROWS - K_ROWS)
        for qi in range(Q_ROWS):
            qrow = r0 + qi
            w0 = min(max(qrow - WIN_H // 2, 0), ROWS - WIN_H)
            for kp in range(K_ROWS // 2):
                halves = []
                for krow in (k0 + 2 * kp, k0 + 2 * kp + 1):
                    in_win = w0 <= krow < w0 + WIN_H
                    halves.append(t_scr[krow - qrow + WIN_H - 1] if in_win else neg)
                o_ref[v, qi * GRID_W:(qi + 1) * GRID_W, kp * 2 * GRID_W:(kp + 1) * 2 * GRID_W] = (
                    jnp.where(left, halves[0], halves[1]))


def _bias_tables(rpb):
    return pl.pallas_call(
        _tbl_kernel,
        grid=(HEADS,),
        in_specs=[pl.BlockSpec(memory_space=pltpu.SMEM)],
        out_specs=pl.BlockSpec((len(TBL_R0), None, Q_BLK, K_BLK), lambda h: (0, h, 0, 0)),
        out_shape=jax.ShapeDtypeStruct((len(TBL_R0), HEADS, Q_BLK, K_BLK), F32),
        scratch_shapes=[pltpu.VMEM((N_DR, GRID_W, 2 * GRID_W), F32)],
        compiler_params=_params(("arbitrary",)),
    )(rpb.reshape(-1))


def _pin_after(value, anchor):
    bits = lambda t: pltpu.bitcast(t, jnp.uint32)
    zero = lax.shift_right_logical(lax.shift_right_logical(bits(anchor), jnp.uint32(16)), jnp.uint32(16))
    return pltpu.bitcast(bits(value) | zero, F32)


def _ffn_tail(x, y, mod_ref, ng2_ref, w1_ref, w2_ref, fg_ref, side_work=None, h2_scr=None):
    x1 = x + mod_ref[:, 2 * D:3 * D] * y
    h2f = _rms_mod(x1, ng2_ref[...], mod_ref[:, 3 * D:4 * D], mod_ref[:, 4 * D:5 * D])
    h2 = h2f.astype(BF16)
    if side_work is not None:
        h2_scr[...] = h2
    acc = jnp.zeros_like(x1)
    token = None
    for c in range(FFN_CHUNKS):
        if side_work is not None:
            if token is not None:
                h2_scr[0:2 * SUBLANES, 0:LANES] = _pin_after(h2f[0:2 * SUBLANES, 0:LANES], token).astype(BF16)
                h2 = h2_scr[...]
            token = side_work(c, y if c == 0 else acc)
        cs = slice(c * FFN_CHUNK, (c + 1) * FFN_CHUNK)
        hid = jnp.maximum(_dot(h2, w1_ref[:, cs]), 0.0)
        acc = acc + _dot((hid * hid).astype(BF16), w2_ref[cs, :])
    x2 = x1 + mod_ref[:, 5 * D:6 * D] * acc
    if fg_ref is not None:
        x2 = x2 * lax.rsqrt(jnp.mean(x2 * x2, axis=-1, keepdims=True) + EPS) * fg_ref[...]
    return x2


def _even_out_kernel(*refs, final):
    if final:
        a_ref, b_ref, x_ref, mod_ref, ng2_ref, wo_ref, w1_ref, w2_ref, fg_ref, o_ref = refs
    else:
        a_ref, b_ref, x_ref, mod_ref, ng2_ref, wo_ref, w1_ref, w2_ref, o_ref = refs
        fg_ref = None
    y = _dot(a_ref[...], wo_ref[0:A_W, :]) + _dot(b_ref[...], wo_ref[A_W:A_W + B_W, :])
    o_ref[...] = _ffn_tail(x_ref[...], y, mod_ref, ng2_ref, w1_ref, w2_ref, fg_ref)


def _even_out(a, b, xs, mod3, layer, ng2, w_out, w1, w2, n_out, final_g=None):
    tok = lambda w: pl.BlockSpec((TILE, w), lambda i: (i, 0))
    in_specs = [tok(A_W), tok(B_W), tok(D), _mod_row_spec(layer), _const_spec((1, D)),
                _const_spec((A_W + B_W, D)), _const_spec((D, FFN)), _const_spec((FFN, D))]
    args = [a, b, xs, mod3, ng2, w_out, w1, w2]
    if final_g is not None:
        in_specs.append(_const_spec((1, D)))
        args.append(final_g)
    return pl.pallas_call(
        functools.partial(_even_out_kernel, final=final_g is not None),
        grid=(n_out // TILE,),
        in_specs=in_specs,
        out_specs=tok(D),
        out_shape=jax.ShapeDtypeStruct((n_out, D), F32),
        compiler_params=_params(("arbitrary",)),
    )(*args)


def _odd_kernel(*refs, final, n_tiles):
    if final:
        (xp_ref, xa_ref, xn_ref, xb_ref, moda_ref, modb_ref, ng1_ref, ng2_ref, wp1_ref, bp1_ref, wdw_ref,
         bdw_ref, lg_ref, lb_ref, wp2_ref, bp2_ref, w1_ref, w2_ref, fg_ref, o_ref, y_scr, conv_scr, w_scr, h2_scr) = refs
    else:
        (xp_ref, xa_ref, xn_ref, xb_ref, moda_ref, modb_ref, ng1_ref, ng2_ref, wp1_ref, bp1_ref, wdw_ref,
         bdw_ref, lg_ref, lb_ref, wp2_ref, bp2_ref, w1_ref, w2_ref, o_ref, y_scr, conv_scr, w_scr, h2_scr) = refs
        fg_ref = None
    i = pl.program_id(0)

    @pl.when(i == 0)
    def _():
        conv_scr[...] = jnp.zeros_like(conv_scr)

    ia = jnp.minimum(i, n_tiles - 1)
    is_lat = ia < LAT_TILES
    pos = ia % TILES_PER_SEQ
    seq_first = jnp.logical_or(jnp.logical_not(is_lat), pos == 0)
    seq_last = jnp.logical_or(jnp.logical_not(is_lat), pos == TILES_PER_SEQ - 1)
    xe = jnp.concatenate([xp_ref[...], xa_ref[...], xn_ref[...]], axis=0)
    h = _rms_mod(xe, ng1_ref[...], moda_ref[:, 0:D], moda_ref[:, D:2 * D]).astype(BF16)
    t = _dot(h, wp1_ref[...]) + bp1_ref[...]
    glu = t[:, 0:D] * jax.nn.sigmoid(t[:, D:2 * D])
    row = lax.broadcasted_iota(jnp.int32, (HALO + TILE + HALO, 1), 0)
    pad = jnp.logical_or(jnp.logical_and(seq_first, row < HALO),
                         jnp.logical_and(seq_last, row >= HALO + TILE))
    y_scr[...] = jnp.where(pad, 0.0, glu)

    acc = conv_scr[...]
    mu = jnp.mean(acc, axis=-1, keepdims=True)
    ac = acc - mu
    var = jnp.mean(ac * ac, axis=-1, keepdims=True)
    ln = ac * lax.rsqrt(var + EPS) * lg_ref[...] + lb_ref[...]
    act = (ln * jax.nn.sigmoid(ln)).astype(BF16)
    y = _dot(act, wp2_ref[...]) + bp2_ref[...]

    base = HALO - CONV_K // 2
    rows_in = CONV_ROWS + SUBLANES

    row_blocks = TILE // CONV_ROWS

    def conv_piece(p, anchor):
        cb, r0 = p // row_blocks, (p % row_blocks) * CONV_ROWS
        lanes = slice(cb * LANES, (cb + 1) * LANES)
        w_scr[p, 0:CONV_K, :] = _pin_after(wdw_ref[:, lanes], anchor[0:1, 0:LANES])
        out = bdw_ref[:, lanes]
        for s in range(SUBLANES):
            part = None
            for off in range(s, base + CONV_K, SUBLANES):
                if off < base:
                    continue
                r1 = r0 + off - s
                term = y_scr[r1:r1 + rows_in, lanes] * w_scr[p, off - base:off - base + 1, :]
                part = term if part is None else part + term
            out = out + part[s:s + CONV_ROWS]
        conv_scr[r0:r0 + CONV_ROWS, lanes] = out
        return out[0:1, :]

    per_slot = CONV_PIECES // FFN_CHUNKS

    def conv_part(slot, anchor):
        for p in range(slot * per_slot, (slot + 1) * per_slot):
            token = conv_piece(p, anchor)
        return token

    o_ref[...] = _ffn_tail(xb_ref[...], y, modb_ref, ng2_ref, w1_ref, w2_ref, fg_ref,
                           side_work=conv_part, h2_scr=h2_scr)


def _odd_layer(xs, mod3, layer, ng1, ng2, wp1, bp1, wdw, bdw, lg, lb, wp2, bp2, w1, w2, final_g=None):
    n = xs.shape[0]
    n_tiles = n // TILE
    per = TILE // HALO
    halo_blocks = n // HALO
    tile_a = lambda i: jnp.minimum(i, n_tiles - 1)
    tile_b = lambda i: jnp.maximum(i - 1, 0)
    in_specs = [
        pl.BlockSpec((HALO, D), lambda i: (jnp.maximum(tile_a(i) * per - 1, 0), 0)),
        pl.BlockSpec((TILE, D), lambda i: (tile_a(i), 0)),
        pl.BlockSpec((HALO, D), lambda i: (jnp.minimum((tile_a(i) + 1) * per, halo_blocks - 1), 0)),
        pl.BlockSpec((TILE, D), lambda i: (tile_b(i), 0)),
        _mod_row_spec(layer, tile_a), _mod_row_spec(layer, tile_b), _const_spec((1, D)), _const_spec((1, D)),
        _const_spec((D, 2 * D)), _const_spec((1, 2 * D)), _const_spec((CONV_K, D)), _const_spec((1, D)),
        _const_spec((1, D)), _const_spec((1, D)), _const_spec((D, D)), _const_spec((1, D)),
        _const_spec((D, FFN)), _const_spec((FFN, D)),
    ]
    args = [xs, xs, xs, xs, mod3, mod3, ng1, ng2, wp1, bp1, wdw, bdw, lg, lb, wp2, bp2, w1, w2]
    if final_g is not None:
        in_specs.append(_const_spec((1, D)))
        args.append(final_g)
    return pl.pallas_call(
        functools.partial(_odd_kernel, final=final_g is not None, n_tiles=n_tiles),
        grid=(n_tiles + 1,),
        in_specs=in_specs,
        out_specs=pl.BlockSpec((TILE, D), lambda i: (tile_b(i), 0)),
        out_shape=jax.ShapeDtypeStruct((n, D), F32),
        scratch_shapes=[pltpu.VMEM((HALO + TILE + HALO, D), F32), pltpu.VMEM((TILE, D), F32),
                        pltpu.VMEM((CONV_PIECES, CONV_K + 1, LANES), F32), pltpu.VMEM((TILE, D), BF16)],
        compiler_params=_params(("arbitrary",)),
    )(*args)


def kernel(x, c, ctx, c_ctx, w_mod, b_mod, norm_g, w_in, w_out, ln_v_g, w_sp, b_sp, rpb, w_pw1, b_pw1,
           w_dw, b_dw, ln_c_g, ln_c_b, w_pw2, b_pw2, w_ff1, w_ff2, final_g):
    row = lambda t: t.reshape(1, -1)
    cc = jnp.concatenate([c, c_ctx[None, :], jnp.zeros((MOD_ROWS - BATCH - 1, D), F32)], axis=0)
    mod3 = _modulation(cc, w_mod, b_mod).reshape(DEPTH * MOD_ROWS, 1, 6 * D)
    xs = jnp.concatenate([x.reshape(T_LAT, D), ctx.reshape(T_CTX, D)], axis=0)
    last_reader = ((DEPTH - 1) // 2) * 2
    for l in range(DEPTH):
        ctx_out = l < last_reader
        fg = row(final_g) if l == DEPTH - 1 else None
        w1 = w_ff1[l].astype(BF16)
        w2 = w_ff2[l].astype(BF16)
        j = l // 2
        if l % 2 == 0:
            a, q, k, v = _even_in(xs, mod3, l, row(norm_g[l, 0]), w_in[j].astype(BF16), row(ln_v_g[j]),
                                  w_sp[j].astype(BF16), b_sp[j][:, :, None])
            b = _attention(q, k, v, _bias_tables(rpb[j]), ctx_out)
            xs = _even_out(a, b, xs, mod3, l, row(norm_g[l, 1]), w_out[j].astype(BF16), w1, w2,
                           T_ALL if ctx_out else T_LAT, fg)
        else:
            if not ctx_out and xs.shape[0] != T_LAT:
                xs = xs[:T_LAT]
            xs = _odd_layer(xs, mod3, l, row(norm_g[l, 0]), row(norm_g[l, 1]), w_pw1[j].astype(BF16),
                            row(b_pw1[j]), w_dw[j], row(b_dw[j]), row(ln_c_g[j]), row(ln_c_b[j]),
                            w_pw2[j].astype(BF16), row(b_pw2[j]), w1, w2, fg)
    return xs[:T_LAT].reshape(BATCH, SEQ, D)
```

```python
import functools

import jax
import jax.numpy as jnp
from jax import lax
from jax.experimental import pallas as pl
from jax.experimental.pallas import tpu as pltpu

D = 1024
BATCH = 4
SEQ = 4096
DEPTH = 4
GRID_W = 64
ROWS = SEQ // GRID_W
CTX_LEN = 256
CHUNK = 128
A_GROUPS = 4
A_W = 512
B_W = 512
HEAD_DIM = 64
HEADS = 8
WIN_H = 8
WIN_W = 16
Q_ROWS = 4
Q_BLK = Q_ROWS * GRID_W
K_ROWS = Q_ROWS + WIN_H
K_BLK = K_ROWS * GRID_W
ROW_STEPS = ROWS // Q_ROWS
MIX_IN = 2 * A_W + 3 * B_W
CONV_K = 31
HALO = 16
SUBLANES = 8
LANES = 128
CONV_ROWS = 128
FFN = 4 * D
FFN_CHUNK = 1024
FFN_CHUNKS = FFN // FFN_CHUNK
EPS = 1e-6
NEG = -1e30

T_LAT = BATCH * SEQ
T_CTX = BATCH * CTX_LEN
T_ALL = T_LAT + T_CTX
TILE = 256
IN_TILE = 512
CONV_PIECES = (D // LANES) * (TILE // CONV_ROWS)
TILES_PER_SEQ = SEQ // TILE
LAT_TILES = T_LAT // TILE
MOD_ROWS = 8
MOD_N = 768
VMEM_LIMIT = 56 * 1024 * 1024

BF16 = jnp.bfloat16
F32 = jnp.float32


def _const_spec(shape):
    zeros = (0,) * len(shape)
    return pl.BlockSpec(shape, lambda *_: zeros, pipeline_mode=pl.Buffered(1))


def _layer_spec(stacked, idx):
    shape = stacked.shape[1:]
    zeros = (0,) * len(shape)
    return pl.BlockSpec((None,) + shape, lambda *_: (idx,) + zeros, pipeline_mode=pl.Buffered(1))


def _params(sem):
    return pltpu.CompilerParams(dimension_semantics=sem, vmem_limit_bytes=VMEM_LIMIT)


def _mod_row_spec(layer, tile_of=lambda i: i, tiles_per_seq=TILES_PER_SEQ):
    return pl.BlockSpec(
        (None, 1, 6 * D),
        lambda i: (layer * MOD_ROWS + jnp.minimum(tile_of(i) // tiles_per_seq, BATCH), 0, 0))


def _rms_mod(x, g, shift, scale):
    y = x * lax.rsqrt(jnp.mean(x * x, axis=-1, keepdims=True) + EPS) * g
    return y * (1.0 + scale) + shift


def _dot(a, b):
    return jnp.dot(a, b, preferred_element_type=F32)


def _dot_nt(a, b):
    return lax.dot_general(a, b, (((1,), (1,)), ((), ())), preferred_element_type=F32)


def _mod_kernel(cc_ref, w_ref, b_ref, o_ref):
    cc = cc_ref[...]
    sc = (cc * jax.nn.sigmoid(cc)).astype(BF16)
    o_ref[...] = _dot(sc, w_ref[...].astype(BF16)) + b_ref[...]


def _modulation(cc, w_mod, b_mod):
    nblk = (6 * D) // MOD_N
    return pl.pallas_call(
        _mod_kernel,
        grid=(DEPTH, nblk),
        in_specs=[
            pl.BlockSpec((MOD_ROWS, D), lambda l, n: (0, 0)),
            pl.BlockSpec((None, D, MOD_N), lambda l, n: (l, 0, n)),
            pl.BlockSpec((None, 1, MOD_N), lambda l, n: (l, 0, n)),
        ],
        out_specs=pl.BlockSpec((None, MOD_ROWS, MOD_N), lambda l, n: (l, 0, n)),
        out_shape=jax.ShapeDtypeStruct((DEPTH, MOD_ROWS, 6 * D), F32),
        compiler_params=_params(("arbitrary", "arbitrary")),
    )(cc, w_mod, b_mod.reshape(DEPTH, 1, 6 * D))


def _load_stream(x_refs, lat_tiles):
    if len(x_refs) == 1:
        return x_refs[0][...]
    return jnp.where(pl.program_id(0) < lat_tiles, x_refs[0][...], x_refs[1][...])


def _stream_specs(xs, tile):
    if not isinstance(xs, tuple):
        return [pl.BlockSpec((tile, D), lambda i: (i, 0))], [xs]
    lat_tiles = xs[0].shape[0] // tile
    return [pl.BlockSpec((tile, D), lambda i: (jnp.minimum(i, lat_tiles - 1), 0)),
            pl.BlockSpec((tile, D), lambda i: (jnp.maximum(i - lat_tiles, 0), 0))], list(xs)


def _even_in_kernel(*refs, n_x):
    x_refs = refs[:n_x]
    mod_ref, ng_ref, w_in_ref, lng_ref, w_sp_ref, b_sp_ref, a_ref, q_ref, k_ref, v_ref = refs[n_x:]
    x = _load_stream(x_refs, T_LAT // IN_TILE)
    h = _rms_mod(x, ng_ref[...], mod_ref[:, 0:D], mod_ref[:, D:2 * D]).astype(BF16)
    z = jax.nn.gelu(_dot(h, w_in_ref[:, 0:2 * A_W]))
    gd = A_W // A_GROUPS
    for grp in range(A_GROUPS):
        g = z[:, A_W + grp * gd:A_W + (grp + 1) * gd]
        mu = jnp.mean(g, axis=-1, keepdims=True)
        gc = g - mu
        var = jnp.mean(gc * gc, axis=-1, keepdims=True)
        gn = (gc * lax.rsqrt(var + EPS) * lng_ref[:, grp * gd:(grp + 1) * gd]).astype(BF16)
        for ch in range(IN_TILE // CHUNK):
            rows = slice(ch * CHUNK, (ch + 1) * CHUNK)
            s = _dot(w_sp_ref[grp], gn[rows]) + b_sp_ref[grp]
            a_ref[rows, grp * gd:(grp + 1) * gd] = (z[rows, grp * gd:(grp + 1) * gd] * s).astype(BF16)
    c0 = 2 * A_W
    q_ref[...] = (_dot(h, w_in_ref[:, c0:c0 + B_W]) * (HEAD_DIM ** -0.5)).astype(BF16)
    k_ref[...] = _dot(h, w_in_ref[:, c0 + B_W:c0 + 2 * B_W]).astype(BF16)
    v_ref[...] = _dot(h, w_in_ref[:, c0 + 2 * B_W:c0 + 3 * B_W]).astype(BF16)


def _even_in(xs, mod3, layer, ng, w_in_all, j, lng, w_sp_all, b_sp):
    x_specs, x_args = _stream_specs(xs, IN_TILE)
    tok = lambda w: pl.BlockSpec((IN_TILE, w), lambda i: (i, 0))
    out = jax.ShapeDtypeStruct((T_ALL, A_W), BF16)
    return pl.pallas_call(
        functools.partial(_even_in_kernel, n_x=len(x_args)),
        grid=(T_ALL // IN_TILE,),
        in_specs=x_specs + [
            _mod_row_spec(layer, tiles_per_seq=SEQ // IN_TILE), _const_spec((1, D)), _layer_spec(w_in_all, j),
            _const_spec((1, A_W)), _layer_spec(w_sp_all, j), _const_spec((A_GROUPS, CHUNK, 1)),
        ],
        out_specs=[tok(A_W)] * 4,
        out_shape=[out] * 4,
        compiler_params=_params(("arbitrary",)),
    )(*x_args, mod3, ng, w_in_all, lng, w_sp_all, b_sp)


def _attn_kernel(q_ref, k_ref, v_ref, kc_ref, vc_ref, tbl_ref, o_ref, *, ctx_queries):
    r = pl.program_id(1)
    lane = lax.broadcasted_iota(jnp.int32, (Q_BLK, 2 * HEAD_DIM), 1)
    first_head = lane < HEAD_DIM

    def attend(local):
        if local:
            start = pl.multiple_of(
                jnp.clip(r * Q_ROWS - WIN_H // 2, 0, ROWS - K_ROWS) * GRID_W, GRID_W)
        for hp in range(HEADS // 2):
            cols = slice(hp * 2 * HEAD_DIM, (hp + 1) * 2 * HEAD_DIM)
            q2 = q_ref[:, cols]
            kc2 = kc_ref[:, cols]
            vc2 = vc_ref[:, cols]
            if local:
                k2 = k_ref[pl.ds(start, K_BLK), cols]
                v2 = v_ref[pl.ds(start, K_BLK), cols]
            outs = []
            for half in range(2):
                qm = jnp.where(first_head if half == 0 else ~first_head, q2, jnp.zeros_like(q2))
                s_c = _dot_nt(qm, kc2)
                m = jnp.max(s_c, axis=-1, keepdims=True)
                if local:
                    s_l = _dot_nt(qm, k2) + tbl_ref[2 * hp + half]
                    m = jnp.maximum(m, jnp.max(s_l, axis=-1, keepdims=True))
                p_c = jnp.exp(s_c - m)
                den = jnp.sum(p_c, axis=-1, keepdims=True)
                o = _dot(p_c.astype(BF16), vc2)
                if local:
                    p_l = jnp.exp(s_l - m)
                    den = den + jnp.sum(p_l, axis=-1, keepdims=True)
                    o = o + _dot(p_l.astype(BF16), v2)
                outs.append(o / den)
            o_ref[:, cols] = jnp.where(first_head, outs[0], outs[1]).astype(BF16)

    if ctx_queries:
        pl.when(r < ROW_STEPS)(lambda: attend(True))
        pl.when(r >= ROW_STEPS)(lambda: attend(False))
    else:
        attend(True)


def _attention(q, k, v, tbl, ctx_queries):
    steps = ROW_STEPS + (1 if ctx_queries else 0)
    lat_blocks = T_LAT // Q_BLK

    def q_idx(b, r):
        return (jnp.where(r < ROW_STEPS, b * ROW_STEPS + r, lat_blocks + b), 0)

    def tbl_idx(b, r):
        return (jnp.where(r == 0, 0, jnp.where(r >= ROW_STEPS - 1, 2, 1)), 0, 0, 0)

    ctx_blk = T_LAT // CTX_LEN
    return pl.pallas_call(
        functools.partial(_attn_kernel, ctx_queries=ctx_queries),
        grid=(BATCH, steps),
        in_specs=[
            pl.BlockSpec((Q_BLK, B_W), q_idx),
            pl.BlockSpec((SEQ, B_W), lambda b, r: (b, 0)),
            pl.BlockSpec((SEQ, B_W), lambda b, r: (b, 0)),
            pl.BlockSpec((CTX_LEN, B_W), lambda b, r: (ctx_blk + b, 0)),
            pl.BlockSpec((CTX_LEN, B_W), lambda b, r: (ctx_blk + b, 0)),
            pl.BlockSpec((None, HEADS, Q_BLK, K_BLK), tbl_idx),
        ],
        out_specs=pl.BlockSpec((Q_BLK, B_W), q_idx),
        out_shape=jax.ShapeDtypeStruct((T_ALL if ctx_queries else T_LAT, B_W), BF16),
        compiler_params=_params(("arbitrary", "arbitrary")),
    )(q, k, v, k, v, tbl)


N_DR = 2 * WIN_H - 1
N_DC = 2 * WIN_W - 1
TBL_R0 = (0, Q_ROWS, ROWS - Q_ROWS)


def _tbl_kernel(rpb_ref, o_ref, t_scr):
    h = pl.program_id(0)
    shape = (GRID_W, 2 * GRID_W)
    cq = lax.broadcasted_iota(jnp.int32, shape, 0)
    lane = lax.broadcasted_iota(jnp.int32, shape, 1)
    ck = lane & (GRID_W - 1)
    dc = jnp.clip(ck - cq + WIN_W - 1, 0, N_DC - 1)
    cs = jnp.clip(cq - WIN_W // 2, 0, GRID_W - WIN_W)
    col_ok = (ck >= cs) & (ck < cs + WIN_W)
    neg = jnp.full(shape, NEG, F32)
    for dr in range(N_DR):
        t = neg
        for j in range(N_DC):
            t = jnp.where(dc == j, rpb_ref[(h * N_DR + dr) * N_DC + j], t)
        t_scr[dr] = jnp.where(col_ok, t, NEG)
    left = lane < GRID_W
    for v, r0 in enumerate(TBL_R0):
        k0 = min(max(r0 - WIN_H // 2, 0), ROWS - K_ROWS)
        for qi in range(Q_ROWS):
            qrow = r0 + qi
            w0 = min(max(qrow - WIN_H // 2, 0), ROWS - WIN_H)
            for kp in range(K_ROWS // 2):
                halves = []
                for krow in (k0 + 2 * kp, k0 + 2 * kp + 1):
                    in_win = w0 <= krow < w0 + WIN_H
                    halves.append(t_scr[krow - qrow + WIN_H - 1] if in_win else neg)
                o_ref[v, qi * GRID_W:(qi + 1) * GRID_W, kp * 2 * GRID_W:(kp + 1) * 2 * GRID_W] = (
                    jnp.where(left, halves[0], halves[1]))


def _bias_tables(rpb):
    return pl.pallas_call(
        _tbl_kernel,
        grid=(HEADS,),
        in_specs=[pl.BlockSpec(memory_space=pltpu.SMEM)],
        out_specs=pl.BlockSpec((len(TBL_R0), None, Q_BLK, K_BLK), lambda h: (0, h, 0, 0)),
        out_shape=jax.ShapeDtypeStruct((len(TBL_R0), HEADS, Q_BLK, K_BLK), F32),
        scratch_shapes=[pltpu.VMEM((N_DR, GRID_W, 2 * GRID_W), F32)],
        compiler_params=_params(("arbitrary",)),
    )(rpb.reshape(-1))


def _pin_after(value, anchor):
    bits = lambda t: pltpu.bitcast(t, jnp.uint32)
    zero = lax.shift_right_logical(lax.shift_right_logical(bits(anchor), jnp.uint32(16)), jnp.uint32(16))
    return pltpu.bitcast(bits(value) | zero, F32)


def _ffn_tail(x, y, mod_ref, ng2_ref, w1_ref, w2_ref, fg_ref, side_work=None, h2_scr=None):
    x1 = x + mod_ref[:, 2 * D:3 * D] * y
    h2f = _rms_mod(x1, ng2_ref[...], mod_ref[:, 3 * D:4 * D], mod_ref[:, 4 * D:5 * D])
    h2 = h2f.astype(BF16)
    if side_work is not None:
        h2_scr[...] = h2
    acc = jnp.zeros_like(x1)
    token = None
    for c in range(FFN_CHUNKS):
        if side_work is not None:
            if token is not None:
                h2_scr[0:2 * SUBLANES, 0:LANES] = _pin_after(h2f[0:2 * SUBLANES, 0:LANES], token).astype(BF16)
                h2 = h2_scr[...]
            token = side_work(c, y if c == 0 else acc)
        cs = slice(c * FFN_CHUNK, (c + 1) * FFN_CHUNK)
        hid = jnp.maximum(_dot(h2, w1_ref[:, cs]), 0.0)
        acc = acc + _dot((hid * hid).astype(BF16), w2_ref[cs, :])
    x2 = x1 + mod_ref[:, 5 * D:6 * D] * acc
    if fg_ref is not None:
        x2 = x2 * lax.rsqrt(jnp.mean(x2 * x2, axis=-1, keepdims=True) + EPS) * fg_ref[...]
    return x2


def _even_out_kernel(*refs, final, n_x):
    x_refs, refs = refs[:n_x], refs[n_x:]
    if final:
        a_ref, b_ref, mod_ref, ng2_ref, wo_ref, w1_ref, w2_ref, fg_ref, o_ref = refs
    else:
        a_ref, b_ref, mod_ref, ng2_ref, wo_ref, w1_ref, w2_ref, o_ref = refs
        fg_ref = None
    y = _dot(a_ref[...], wo_ref[0:A_W, :]) + _dot(b_ref[...], wo_ref[A_W:A_W + B_W, :])
    o_ref[...] = _ffn_tail(_load_stream(x_refs, LAT_TILES), y, mod_ref, ng2_ref, w1_ref, w2_ref, fg_ref)


def _even_out(a, b, xs, mod3, layer, ng2, w_out_all, j, w1_all, w2_all, n_out, final_g=None):
    x_specs, x_args = _stream_specs(xs, TILE)
    tok = lambda w: pl.BlockSpec((TILE, w), lambda i: (i, 0))
    in_specs = x_specs + [tok(A_W), tok(B_W), _mod_row_spec(layer), _const_spec((1, D)),
                          _layer_spec(w_out_all, j), _layer_spec(w1_all, layer), _layer_spec(w2_all, layer)]
    args = x_args + [a, b, mod3, ng2, w_out_all, w1_all, w2_all]
    if final_g is not None:
        in_specs.append(_const_spec((1, D)))
        args.append(final_g)
    return pl.pallas_call(
        functools.partial(_even_out_kernel, final=final_g is not None, n_x=len(x_args)),
        grid=(n_out // TILE,),
        in_specs=in_specs,
        out_specs=tok(D),
        out_shape=jax.ShapeDtypeStruct((n_out, D), F32),
        compiler_params=_params(("arbitrary",)),
    )(*args)


def _odd_kernel(*refs, final, n_tiles):
    if final:
        (xp_ref, xa_ref, xn_ref, xb_ref, moda_ref, modb_ref, ng1_ref, ng2_ref, wp1_ref, bp1_ref, wdw_ref,
         bdw_ref, lg_ref, lb_ref, wp2_ref, bp2_ref, w1_ref, w2_ref, fg_ref, o_ref, y_scr, conv_scr, w_scr, h2_scr) = refs
    else:
        (xp_ref, xa_ref, xn_ref, xb_ref, moda_ref, modb_ref, ng1_ref, ng2_ref, wp1_ref, bp1_ref, wdw_ref,
         bdw_ref, lg_ref, lb_ref, wp2_ref, bp2_ref, w1_ref, w2_ref, o_ref, y_scr, conv_scr, w_scr, h2_scr) = refs
        fg_ref = None
    i = pl.program_id(0)

    @pl.when(i == 0)
    def _():
        conv_scr[...] = jnp.zeros_like(conv_scr)

    ia = jnp.minimum(i, n_tiles - 1)
    is_lat = ia < LAT_TILES
    pos = ia % TILES_PER_SEQ
    seq_first = jnp.logical_or(jnp.logical_not(is_lat), pos == 0)
    seq_last = jnp.logical_or(jnp.logical_not(is_lat), pos == TILES_PER_SEQ - 1)
    xe = jnp.concatenate([xp_ref[...], xa_ref[...], xn_ref[...]], axis=0)
    h = _rms_mod(xe, ng1_ref[...], moda_ref[:, 0:D], moda_ref[:, D:2 * D]).astype(BF16)
    t = _dot(h, wp1_ref[...]) + bp1_ref[...]
    glu = t[:, 0:D] * jax.nn.sigmoid(t[:, D:2 * D])
    row = lax.broadcasted_iota(jnp.int32, (HALO + TILE + HALO, 1), 0)
    pad = jnp.logical_or(jnp.logical_and(seq_first, row < HALO),
                         jnp.logical_and(seq_last, row >= HALO + TILE))
    y_scr[...] = jnp.where(pad, 0.0, glu)

    acc = conv_scr[...]
    mu = jnp.mean(acc, axis=-1, keepdims=True)
    ac = acc - mu
    var = jnp.mean(ac * ac, axis=-1, keepdims=True)
    ln = ac * lax.rsqrt(var + EPS) * lg_ref[...] + lb_ref[...]
    act = (ln * jax.nn.sigmoid(ln)).astype(BF16)
    y = _dot(act, wp2_ref[...]) + bp2_ref[...]

    base = HALO - CONV_K // 2
    rows_in = CONV_ROWS + SUBLANES

    row_blocks = TILE // CONV_ROWS

    def conv_piece(p, anchor):
        cb, r0 = p // row_blocks, (p % row_blocks) * CONV_ROWS
        lanes = slice(cb * LANES, (cb + 1) * LANES)
        w_scr[p, 0:CONV_K, :] = _pin_after(wdw_ref[:, lanes], anchor[0:1, 0:LANES])
        out = bdw_ref[:, lanes]
        for s in range(SUBLANES):
            part = None
            for off in range(s, base + CONV_K, SUBLANES):
                if off < base:
                    continue
                r1 = r0 + off - s
                term = y_scr[r1:r1 + rows_in, lanes] * w_scr[p, off - base:off - base + 1, :]
                part = term if part is None else part + term
            out = out + part[s:s + CONV_ROWS]
        conv_scr[r0:r0 + CONV_ROWS, lanes] = out
        return out[0:1, :]

    per_slot = CONV_PIECES // FFN_CHUNKS

    def conv_part(slot, anchor):
        for p in range(slot * per_slot, (slot + 1) * per_slot):
            token = conv_piece(p, anchor)
        return token

    o_ref[...] = _ffn_tail(xb_ref[...], y, modb_ref, ng2_ref, w1_ref, w2_ref, fg_ref,
                           side_work=conv_part, h2_scr=h2_scr)


def _odd_layer(xs, mod3, layer, ng1, ng2, wp1_all, j, bp1, wdw, bdw, lg, lb, wp2_all, bp2, w1_all, w2_all,
               final_g=None):
    n = xs.shape[0]
    n_tiles = n // TILE
    per = TILE // HALO
    halo_blocks = n // HALO
    tile_a = lambda i: jnp.minimum(i, n_tiles - 1)
    tile_b = lambda i: jnp.maximum(i - 1, 0)
    in_specs = [
        pl.BlockSpec((HALO, D), lambda i: (jnp.maximum(tile_a(i) * per - 1, 0), 0)),
        pl.BlockSpec((TILE, D), lambda i: (tile_a(i), 0)),
        pl.BlockSpec((HALO, D), lambda i: (jnp.minimum((tile_a(i) + 1) * per, halo_blocks - 1), 0)),
        pl.BlockSpec((TILE, D), lambda i: (tile_b(i), 0)),
        _mod_row_spec(layer, tile_a), _mod_row_spec(layer, tile_b), _const_spec((1, D)), _const_spec((1, D)),
        _layer_spec(wp1_all, j), _const_spec((1, 2 * D)), _const_spec((CONV_K, D)), _const_spec((1, D)),
        _const_spec((1, D)), _const_spec((1, D)), _layer_spec(wp2_all, j), _const_spec((1, D)),
        _layer_spec(w1_all, layer), _layer_spec(w2_all, layer),
    ]
    args = [xs, xs, xs, xs, mod3, mod3, ng1, ng2, wp1_all, bp1, wdw, bdw, lg, lb, wp2_all, bp2, w1_all, w2_all]
    if final_g is not None:
        in_specs.append(_const_spec((1, D)))
        args.append(final_g)
    return pl.pallas_call(
        functools.partial(_odd_kernel, final=final_g is not None, n_tiles=n_tiles),
        grid=(n_tiles + 1,),
        in_specs=in_specs,
        out_specs=pl.BlockSpec((TILE, D), lambda i: (tile_b(i), 0)),
        out_shape=jax.ShapeDtypeStruct((n, D), F32),
        scratch_shapes=[pltpu.VMEM((HALO + TILE + HALO, D), F32), pltpu.VMEM((TILE, D), F32),
                        pltpu.VMEM((CONV_PIECES, CONV_K + 1, LANES), F32), pltpu.VMEM((TILE, D), BF16)],
        compiler_params=_params(("arbitrary",)),
    )(*args)


def kernel(x, c, ctx, c_ctx, w_mod, b_mod, norm_g, w_in, w_out, ln_v_g, w_sp, b_sp, rpb, w_pw1, b_pw1,
           w_dw, b_dw, ln_c_g, ln_c_b, w_pw2, b_pw2, w_ff1, w_ff2, final_g):
    row = lambda t: t.reshape(1, -1)
    cc = jnp.concatenate([c, c_ctx[None, :], jnp.zeros((MOD_ROWS - BATCH - 1, D), F32)], axis=0)
    mod3 = _modulation(cc, w_mod, b_mod).reshape(DEPTH * MOD_ROWS, 1, 6 * D)
    xs = (x.reshape(T_LAT, D), ctx.reshape(T_CTX, D))
    w_in, w_out, w_sp, w_pw1, w_pw2, w_ff1, w_ff2 = (
        w.astype(BF16) for w in (w_in, w_out, w_sp, w_pw1, w_pw2, w_ff1, w_ff2))
    last_reader = ((DEPTH - 1) // 2) * 2
    for l in range(DEPTH):
        ctx_out = l < last_reader
        fg = row(final_g) if l == DEPTH - 1 else None
        j = l // 2
        if l % 2 == 0:
            a, q, k, v = _even_in(xs, mod3, l, row(norm_g[l, 0]), w_in, j, row(ln_v_g[j]),
                                  w_sp, b_sp[j][:, :, None])
            b = _attention(q, k, v, _bias_tables(rpb[j]), ctx_out)
            xs = _even_out(a, b, xs, mod3, l, row(norm_g[l, 1]), w_out, j, w_ff1, w_ff2,
                           T_ALL if ctx_out else T_LAT, fg)
        else:
            xs = _odd_layer(xs, mod3, l, row(norm_g[l, 0]), row(norm_g[l, 1]), w_pw1, j,
                            row(b_pw1[j]), w_dw[j], row(b_dw[j]), row(ln_c_g[j]), row(ln_c_b[j]),
                            w_pw2, row(b_pw2[j]), w_ff1, w_ff2, fg)
    return xs.reshape(BATCH, SEQ, D)
```

```python
import functools

import jax
import jax.numpy as jnp
from jax import lax
from jax.experimental import pallas as pl
from jax.experimental.pallas import tpu as pltpu

D = 1024
BATCH = 4
SEQ = 4096
DEPTH = 4
GRID_W = 64
ROWS = SEQ // GRID_W
CTX_LEN = 256
CHUNK = 128
A_GROUPS = 4
A_W = 512
B_W = 512
HEAD_DIM = 64
HEADS = 8
WIN_H = 8
WIN_W = 16
Q_ROWS = 4
Q_BLK = Q_ROWS * GRID_W
K_ROWS = Q_ROWS + WIN_H
K_BLK = K_ROWS * GRID_W
ROW_STEPS = ROWS // Q_ROWS
MIX_IN = 2 * A_W + 3 * B_W
CONV_K = 31
HALO = 16
SUBLANES = 8
LANES = 128
CONV_ROWS = 128
FFN = 4 * D
FFN_CHUNK = 1024
FFN_CHUNKS = FFN // FFN_CHUNK
EPS = 1e-6
NEG = -1e30

T_LAT = BATCH * SEQ
T_CTX = BATCH * CTX_LEN
T_ALL = T_LAT + T_CTX
TILE = 256
IN_TILE = 1024
OUT_TILE = 512
CONV_PIECES = (D // LANES) * (TILE // CONV_ROWS)
TILES_PER_SEQ = SEQ // TILE
LAT_TILES = T_LAT // TILE
MOD_ROWS = 8
MOD_N = 768
VMEM_LIMIT = 56 * 1024 * 1024

BF16 = jnp.bfloat16
F32 = jnp.float32


def _const_spec(shape):
    zeros = (0,) * len(shape)
    return pl.BlockSpec(shape, lambda *_: zeros, pipeline_mode=pl.Buffered(1))


def _layer_spec(stacked, idx):
    shape = stacked.shape[1:]
    zeros = (0,) * len(shape)
    return pl.BlockSpec((None,) + shape, lambda *_: (idx,) + zeros, pipeline_mode=pl.Buffered(1))


def _params(sem):
    return pltpu.CompilerParams(dimension_semantics=sem, vmem_limit_bytes=VMEM_LIMIT)


def _mod_row_spec(layer, tile_of=lambda i: i, tiles_per_seq=TILES_PER_SEQ):
    return pl.BlockSpec(
        (None, 1, 6 * D),
        lambda i: (layer * MOD_ROWS + jnp.minimum(tile_of(i) // tiles_per_seq, BATCH), 0, 0))


def _rms_mod(x, g, shift, scale):
    return x * lax.rsqrt(jnp.mean(x * x, axis=-1, keepdims=True) + EPS) * (g * (1.0 + scale)) + shift


def _dot(a, b):
    return jnp.dot(a, b, preferred_element_type=F32)


def _dot_nt(a, b):
    return lax.dot_general(a, b, (((1,), (1,)), ((), ())), preferred_element_type=F32)


def _mod_kernel(cc_ref, w_ref, b_ref, o_ref):
    cc = cc_ref[...]
    sc = (cc * jax.nn.sigmoid(cc)).astype(BF16)
    o_ref[...] = _dot(sc, w_ref[...].astype(BF16)) + b_ref[...]


def _modulation(cc, w_mod, b_mod):
    nblk = (6 * D) // MOD_N
    return pl.pallas_call(
        _mod_kernel,
        grid=(DEPTH, nblk),
        in_specs=[
            pl.BlockSpec((MOD_ROWS, D), lambda l, n: (0, 0)),
            pl.BlockSpec((None, D, MOD_N), lambda l, n: (l, 0, n)),
            pl.BlockSpec((None, 1, MOD_N), lambda l, n: (l, 0, n)),
        ],
        out_specs=pl.BlockSpec((None, MOD_ROWS, MOD_N), lambda l, n: (l, 0, n)),
        out_shape=jax.ShapeDtypeStruct((DEPTH, MOD_ROWS, 6 * D), F32),
        compiler_params=_params(("arbitrary", "arbitrary")),
    )(cc, w_mod, b_mod.reshape(DEPTH, 1, 6 * D))


def _load_stream(x_refs, lat_tiles):
    if len(x_refs) == 1:
        return x_refs[0][...]
    return jnp.where(pl.program_id(0) < lat_tiles, x_refs[0][...], x_refs[1][...])


def _stream_specs(xs, tile):
    if not isinstance(xs, tuple):
        return [pl.BlockSpec((tile, D), lambda i: (i, 0))], [xs]
    lat_tiles = xs[0].shape[0] // tile
    return [pl.BlockSpec((tile, D), lambda i: (jnp.minimum(i, lat_tiles - 1), 0)),
            pl.BlockSpec((tile, D), lambda i: (jnp.maximum(i - lat_tiles, 0), 0))], list(xs)


def _even_in_kernel(*refs, n_x):
    x_refs = refs[:n_x]
    mod_ref, ng_ref, w_in_ref, lng_ref, w_sp_ref, b_sp_ref, a_ref, q_ref, k_ref, v_ref = refs[n_x:]
    x = _load_stream(x_refs, T_LAT // IN_TILE)
    h = _rms_mod(x, ng_ref[...], mod_ref[:, 0:D], mod_ref[:, D:2 * D]).astype(BF16)
    z = jax.nn.gelu(_dot(h, w_in_ref[:, 0:2 * A_W]))
    gd = A_W // A_GROUPS
    for grp in range(A_GROUPS):
        g = z[:, A_W + grp * gd:A_W + (grp + 1) * gd]
        mu = jnp.mean(g, axis=-1, keepdims=True)
        gc = g - mu
        var = jnp.mean(gc * gc, axis=-1, keepdims=True)
        gn = (gc * lax.rsqrt(var + EPS) * lng_ref[:, grp * gd:(grp + 1) * gd]).astype(BF16)
        for ch in range(IN_TILE // CHUNK):
            rows = slice(ch * CHUNK, (ch + 1) * CHUNK)
            s = _dot(w_sp_ref[grp], gn[rows]) + b_sp_ref[grp]
            a_ref[rows, grp * gd:(grp + 1) * gd] = (z[rows, grp * gd:(grp + 1) * gd] * s).astype(BF16)
    c0 = 2 * A_W
    q_ref[...] = (_dot(h, w_in_ref[:, c0:c0 + B_W]) * (HEAD_DIM ** -0.5)).astype(BF16)
    k_ref[...] = _dot(h, w_in_ref[:, c0 + B_W:c0 + 2 * B_W]).astype(BF16)
    v_ref[...] = _dot(h, w_in_ref[:, c0 + 2 * B_W:c0 + 3 * B_W]).astype(BF16)


def _even_in(xs, mod3, layer, ng, w_in_all, j, lng, w_sp_all, b_sp):
    x_specs, x_args = _stream_specs(xs, IN_TILE)
    tok = lambda w: pl.BlockSpec((IN_TILE, w), lambda i: (i, 0))
    out = jax.ShapeDtypeStruct((T_ALL, A_W), BF16)
    return pl.pallas_call(
        functools.partial(_even_in_kernel, n_x=len(x_args)),
        grid=(T_ALL // IN_TILE,),
        in_specs=x_specs + [
            _mod_row_spec(layer, tiles_per_seq=SEQ // IN_TILE), _const_spec((1, D)), _layer_spec(w_in_all, j),
            _const_spec((1, A_W)), _layer_spec(w_sp_all, j), _const_spec((A_GROUPS, CHUNK, 1)),
        ],
        out_specs=[tok(A_W)] * 4,
        out_shape=[out] * 4,
        compiler_params=_params(("arbitrary",)),
    )(*x_args, mod3, ng, w_in_all, lng, w_sp_all, b_sp)


def _attn_kernel(q_ref, k_ref, v_ref, kc_ref, vc_ref, tbl_ref, o_ref, *, ctx_queries):
    r = pl.program_id(1)
    lane = lax.broadcasted_iota(jnp.int32, (Q_BLK, 2 * HEAD_DIM), 1)
    first_head = lane < HEAD_DIM

    def attend(local):
        if local:
            start = pl.multiple_of(
                jnp.clip(r * Q_ROWS - WIN_H // 2, 0, ROWS - K_ROWS) * GRID_W, GRID_W)
        for hp in range(HEADS // 2):
            cols = slice(hp * 2 * HEAD_DIM, (hp + 1) * 2 * HEAD_DIM)
            q2 = q_ref[:, cols]
            k2 = kc_ref[:, cols]
            v2 = vc_ref[:, cols]
            if local:
                k2 = jnp.concatenate([k_ref[pl.ds(start, K_BLK), cols], k2], axis=0)
                v2 = jnp.concatenate([v_ref[pl.ds(start, K_BLK), cols], v2], axis=0)
            outs = []
            for half in range(2):
                qm = jnp.where(first_head if half == 0 else ~first_head, q2, jnp.zeros_like(q2))
                s = _dot_nt(qm, k2)
                if local:
                    s = jnp.concatenate([s[:, 0:K_BLK] + tbl_ref[2 * hp + half], s[:, K_BLK:]], axis=1)
                m = jnp.max(s, axis=-1, keepdims=True)
                p = jnp.exp(s - m)
                den = jnp.sum(p, axis=-1, keepdims=True)
                outs.append(_dot(p.astype(BF16), v2) / den)
            o_ref[:, cols] = jnp.where(first_head, outs[0], outs[1]).astype(BF16)

    if ctx_queries:
        pl.when(r < ROW_STEPS)(lambda: attend(True))
        pl.when(r >= ROW_STEPS)(lambda: attend(False))
    else:
        attend(True)


def _attention(q, k, v, tbl, ctx_queries):
    steps = ROW_STEPS + (1 if ctx_queries else 0)
    lat_blocks = T_LAT // Q_BLK

    def q_idx(b, r):
        return (jnp.where(r < ROW_STEPS, b * ROW_STEPS + r, lat_blocks + b), 0)

    def tbl_idx(b, r):
        return (jnp.where(r == 0, 0, jnp.where(r >= ROW_STEPS - 1, 2, 1)), 0, 0, 0)

    ctx_blk = T_LAT // CTX_LEN
    return pl.pallas_call(
        functools.partial(_attn_kernel, ctx_queries=ctx_queries),
        grid=(BATCH, steps),
        in_specs=[
            pl.BlockSpec((Q_BLK, B_W), q_idx),
            pl.BlockSpec((SEQ, B_W), lambda b, r: (b, 0)),
            pl.BlockSpec((SEQ, B_W), lambda b, r: (b, 0)),
            pl.BlockSpec((CTX_LEN, B_W), lambda b, r: (ctx_blk + b, 0)),
            pl.BlockSpec((CTX_LEN, B_W), lambda b, r: (ctx_blk + b, 0)),
            pl.BlockSpec((None, HEADS, Q_BLK, K_BLK), tbl_idx),
        ],
        out_specs=pl.BlockSpec((Q_BLK, B_W), q_idx),
        out_shape=jax.ShapeDtypeStruct((T_ALL if ctx_queries else T_LAT, B_W), BF16),
        compiler_params=_params(("arbitrary", "arbitrary")),
    )(q, k, v, k, v, tbl)


N_DR = 2 * WIN_H - 1
N_DC = 2 * WIN_W - 1
TBL_R0 = (0, Q_ROWS, ROWS - Q_ROWS)


def _tbl_kernel(rpb_ref, o_ref, t_scr):
    h = pl.program_id(0)
    shape = (GRID_W, 2 * GRID_W)
    cq = lax.broadcasted_iota(jnp.int32, shape, 0)
    lane = lax.broadcasted_iota(jnp.int32, shape, 1)
    ck = lane & (GRID_W - 1)
    dc = jnp.clip(ck - cq + WIN_W - 1, 0, N_DC - 1)
    cs = jnp.clip(cq - WIN_W // 2, 0, GRID_W - WIN_W)
    col_ok = (ck >= cs) & (ck < cs + WIN_W)
    neg = jnp.full(shape, NEG, F32)
    for dr in range(N_DR):
        t = neg
        for j in range(N_DC):
            t = jnp.where(dc == j, rpb_ref[(h * N_DR + dr) * N_DC + j], t)
        t_scr[dr] = jnp.where(col_ok, t, NEG)
    left = lane < GRID_W
    for v, r0 in enumerate(TBL_R0):
        k0 = min(max(r0 - WIN_H // 2, 0), ROWS - K_ROWS)
        for qi in range(Q_ROWS):
            qrow = r0 + qi
            w0 = min(max(qrow - WIN_H // 2, 0), ROWS - WIN_H)
            for kp in range(K_ROWS // 2):
                halves = []
                for krow in (k0 + 2 * kp, k0 + 2 * kp + 1):
                    in_win = w0 <= krow < w0 + WIN_H
                    halves.append(t_scr[krow - qrow + WIN_H - 1] if in_win else neg)
                o_ref[v, qi * GRID_W:(qi + 1) * GRID_W, kp * 2 * GRID_W:(kp + 1) * 2 * GRID_W] = (
                    jnp.where(left, halves[0], halves[1]))


def _bias_tables(rpb):
    return pl.pallas_call(
        _tbl_kernel,
        grid=(HEADS,),
        in_specs=[pl.BlockSpec(memory_space=pltpu.SMEM)],
        out_specs=pl.BlockSpec((len(TBL_R0), None, Q_BLK, K_BLK), lambda h: (0, h, 0, 0)),
        out_shape=jax.ShapeDtypeStruct((len(TBL_R0), HEADS, Q_BLK, K_BLK), F32),
        scratch_shapes=[pltpu.VMEM((N_DR, GRID_W, 2 * GRID_W), F32)],
        compiler_params=_params(("arbitrary",)),
    )(rpb.reshape(-1))


def _pin_after(value, anchor):
    bits = lambda t: pltpu.bitcast(t, jnp.uint32)
    zero = lax.shift_right_logical(lax.shift_right_logical(bits(anchor), jnp.uint32(16)), jnp.uint32(16))
    return pltpu.bitcast(bits(value) | zero, F32)


def _ffn_tail(x, y, mod_ref, ng2_ref, w1_ref, w2_ref, fg_ref, side_work=None, h2_scr=None):
    x1 = x + mod_ref[:, 2 * D:3 * D] * y
    h2f = _rms_mod(x1, ng2_ref[...], mod_ref[:, 3 * D:4 * D], mod_ref[:, 4 * D:5 * D])
    h2 = h2f.astype(BF16)
    if side_work is not None:
        h2_scr[...] = h2
    acc = None
    token = None
    for c in range(FFN_CHUNKS):
        if side_work is not None:
            if token is not None:
                h2_scr[0:2 * SUBLANES, 0:LANES] = _pin_after(h2f[0:2 * SUBLANES, 0:LANES], token).astype(BF16)
                h2 = h2_scr[...]
            token = side_work(c, y if c == 0 else acc)
        cs = slice(c * FFN_CHUNK, (c + 1) * FFN_CHUNK)
        hid = jnp.maximum(_dot(h2, w1_ref[:, cs]), 0.0)
        part = _dot((hid * hid).astype(BF16), w2_ref[cs, :])
        acc = part if acc is None else acc + part
    x2 = x1 + mod_ref[:, 5 * D:6 * D] * acc
    if fg_ref is not None:
        x2 = x2 * lax.rsqrt(jnp.mean(x2 * x2, axis=-1, keepdims=True) + EPS) * fg_ref[...]
    return x2


def _even_out_kernel(*refs, final, n_x):
    x_refs, refs = refs[:n_x], refs[n_x:]
    if final:
        a_ref, b_ref, mod_ref, ng2_ref, wo_ref, w1_ref, w2_ref, fg_ref, o_ref = refs
    else:
        a_ref, b_ref, mod_ref, ng2_ref, wo_ref, w1_ref, w2_ref, o_ref = refs
        fg_ref = None
    y = _dot(a_ref[...], wo_ref[0:A_W, :]) + _dot(b_ref[...], wo_ref[A_W:A_W + B_W, :])
    o_ref[...] = _ffn_tail(_load_stream(x_refs, T_LAT // OUT_TILE), y, mod_ref, ng2_ref, w1_ref, w2_ref, fg_ref)


def _even_out(a, b, xs, mod3, layer, ng2, w_out_all, j, w1_all, w2_all, n_out, final_g=None):
    x_specs, x_args = _stream_specs(xs, OUT_TILE)
    tok = lambda w: pl.BlockSpec((OUT_TILE, w), lambda i: (i, 0))
    in_specs = x_specs + [tok(A_W), tok(B_W), _mod_row_spec(layer, tiles_per_seq=SEQ // OUT_TILE),
                          _const_spec((1, D)),
                          _layer_spec(w_out_all, j), _layer_spec(w1_all, layer), _layer_spec(w2_all, layer)]
    args = x_args + [a, b, mod3, ng2, w_out_all, w1_all, w2_all]
    if final_g is not None:
        in_specs.append(_const_spec((1, D)))
        args.append(final_g)
    return pl.pallas_call(
        functools.partial(_even_out_kernel, final=final_g is not None, n_x=len(x_args)),
        grid=(n_out // OUT_TILE,),
        in_specs=in_specs,
        out_specs=tok(D),
        out_shape=jax.ShapeDtypeStruct((n_out, D), F32),
        compiler_params=_params(("arbitrary",)),
    )(*args)


def _odd_kernel(*refs, final, n_tiles):
    if final:
        (xp_ref, xa_ref, xn_ref, xb_ref, moda_ref, modb_ref, ng1_ref, ng2_ref, wp1_ref, bp1_ref, wdw_ref,
         bdw_ref, lg_ref, lb_ref, wp2_ref, bp2_ref, w1_ref, w2_ref, fg_ref, o_ref, y_scr, conv_scr, w_scr, h2_scr) = refs
    else:
        (xp_ref, xa_ref, xn_ref, xb_ref, moda_ref, modb_ref, ng1_ref, ng2_ref, wp1_ref, bp1_ref, wdw_ref,
         bdw_ref, lg_ref, lb_ref, wp2_ref, bp2_ref, w1_ref, w2_ref, o_ref, y_scr, conv_scr, w_scr, h2_scr) = refs
        fg_ref = None
    i = pl.program_id(0)

    @pl.when(i == 0)
    def _():
        conv_scr[...] = jnp.zeros_like(conv_scr)

    ia = jnp.minimum(i, n_tiles - 1)
    is_lat = ia < LAT_TILES
    pos = ia % TILES_PER_SEQ
    seq_first = jnp.logical_or(jnp.logical_not(is_lat), pos == 0)
    seq_last = jnp.logical_or(jnp.logical_not(is_lat), pos == TILES_PER_SEQ - 1)
    xe = jnp.concatenate([xp_ref[...], xa_ref[...], xn_ref[...]], axis=0)
    h = _rms_mod(xe, ng1_ref[...], moda_ref[:, 0:D], moda_ref[:, D:2 * D]).astype(BF16)
    t = _dot(h, wp1_ref[...]) + bp1_ref[...]
    glu = t[:, 0:D] * jax.nn.sigmoid(t[:, D:2 * D])
    y_scr[0:HALO] = jnp.where(seq_first, 0.0, glu[0:HALO])
    y_scr[HALO:HALO + TILE] = glu[HALO:HALO + TILE]
    y_scr[HALO + TILE:] = jnp.where(seq_last, 0.0, glu[HALO + TILE:])

    acc = conv_scr[...]
    mu = jnp.mean(acc, axis=-1, keepdims=True)
    ac = acc - mu
    var = jnp.mean(ac * ac, axis=-1, keepdims=True)
    ln = ac * lax.rsqrt(var + EPS) * lg_ref[...] + lb_ref[...]
    act = (ln * jax.nn.sigmoid(ln)).astype(BF16)
    y = _dot(act, wp2_ref[...]) + bp2_ref[...]

    base = HALO - CONV_K // 2
    rows_in = CONV_ROWS + SUBLANES

    row_blocks = TILE // CONV_ROWS

    def conv_piece(p, anchor):
        cb, r0 = p // row_blocks, (p % row_blocks) * CONV_ROWS
        lanes = slice(cb * LANES, (cb + 1) * LANES)
        w_scr[p, 0:CONV_K, :] = _pin_after(wdw_ref[:, lanes], anchor[0:1, 0:LANES])
        out = bdw_ref[:, lanes]
        for s in range(SUBLANES):
            part = None
            for off in range(s, base + CONV_K, SUBLANES):
                if off < base:
                    continue
                r1 = r0 + off - s
                term = y_scr[r1:r1 + rows_in, lanes] * w_scr[p, off - base:off - base + 1, :]
                part = term if part is None else part + term
            out = out + part[s:s + CONV_ROWS]
        conv_scr[r0:r0 + CONV_ROWS, lanes] = out
        return out[0:1, :]

    per_slot = CONV_PIECES // FFN_CHUNKS

    def conv_part(slot, anchor):
        for p in range(slot * per_slot, (slot + 1) * per_slot):
            token = conv_piece(p, anchor)
        return token

    o_ref[...] = _ffn_tail(xb_ref[...], y, modb_ref, ng2_ref, w1_ref, w2_ref, fg_ref,
                           side_work=conv_part, h2_scr=h2_scr)


def _odd_layer(xs, mod3, layer, ng1, ng2, wp1_all, j, bp1, wdw, bdw, lg, lb, wp2_all, bp2, w1_all, w2_all,
               final_g=None):
    n = xs.shape[0]
    n_tiles = n // TILE
    per = TILE // HALO
    halo_blocks = n // HALO
    tile_a = lambda i: jnp.minimum(i, n_tiles - 1)
    tile_b = lambda i: jnp.maximum(i - 1, 0)
    in_specs = [
        pl.BlockSpec((HALO, D), lambda i: (jnp.maximum(tile_a(i) * per - 1, 0), 0)),
        pl.BlockSpec((TILE, D), lambda i: (tile_a(i), 0)),
        pl.BlockSpec((HALO, D), lambda i: (jnp.minimum((tile_a(i) + 1) * per, halo_blocks - 1), 0)),
        pl.BlockSpec((TILE, D), lambda i: (tile_b(i), 0)),
        _mod_row_spec(layer, tile_a), _mod_row_spec(layer, tile_b), _const_spec((1, D)), _const_spec((1, D)),
        _layer_spec(wp1_all, j), _const_spec((1, 2 * D)), _const_spec((CONV_K, D)), _const_spec((1, D)),
        _const_spec((1, D)), _const_spec((1, D)), _layer_spec(wp2_all, j), _const_spec((1, D)),
        _layer_spec(w1_all, layer), _layer_spec(w2_all, layer),
    ]
    args = [xs, xs, xs, xs, mod3, mod3, ng1, ng2, wp1_all, bp1, wdw, bdw, lg, lb, wp2_all, bp2, w1_all, w2_all]
    if final_g is not None:
        in_specs.append(_const_spec((1, D)))
        args.append(final_g)
    return pl.pallas_call(
        functools.partial(_odd_kernel, final=final_g is not None, n_tiles=n_tiles),
        grid=(n_tiles + 1,),
        in_specs=in_specs,
        out_specs=pl.BlockSpec((TILE, D), lambda i: (tile_b(i), 0)),
        out_shape=jax.ShapeDtypeStruct((n, D), F32),
        scratch_shapes=[pltpu.VMEM((HALO + TILE + HALO, D), F32), pltpu.VMEM((TILE, D), F32),
                        pltpu.VMEM((CONV_PIECES, CONV_K + 1, LANES), F32), pltpu.VMEM((TILE, D), BF16)],
        compiler_params=_params(("arbitrary",)),
    )(*args)


def kernel(x, c, ctx, c_ctx, w_mod, b_mod, norm_g, w_in, w_out, ln_v_g, w_sp, b_sp, rpb, w_pw1, b_pw1,
           w_dw, b_dw, ln_c_g, ln_c_b, w_pw2, b_pw2, w_ff1, w_ff2, final_g):
    row = lambda t: t.reshape(1, -1)
    cc = jnp.concatenate([c, c_ctx[None, :], jnp.zeros((MOD_ROWS - BATCH - 1, D), F32)], axis=0)
    mod3 = _modulation(cc, w_mod, b_mod).reshape(DEPTH * MOD_ROWS, 1, 6 * D)
    xs = (x.reshape(T_LAT, D), ctx.reshape(T_CTX, D))
    w_in, w_out, w_sp, w_pw1, w_pw2, w_ff1, w_ff2 = (
        w.astype(BF16) for w in (w_in, w_out, w_sp, w_pw1, w_pw2, w_ff1, w_ff2))
    last_reader = ((DEPTH - 1) // 2) * 2
    for l in range(DEPTH):
        ctx_out = l < last_reader
        fg = row(final_g) if l == DEPTH - 1 else None
        j = l // 2
        if l % 2 == 0:
            a, q, k, v = _even_in(xs, mod3, l, row(norm_g[l, 0]), w_in, j, row(ln_v_g[j]),
                                  w_sp, b_sp[j][:, :, None])
            b = _attention(q, k, v, _bias_tables(rpb[j]), ctx_out)
            xs = _even_out(a, b, xs, mod3, l, row(norm_g[l, 1]), w_out, j, w_ff1, w_ff2,
                           T_ALL if ctx_out else T_LAT, fg)
        else:
            xs = _odd_layer(xs, mod3, l, row(norm_g[l, 0]), row(norm_g[l, 1]), w_pw1, j,
                            row(b_pw1[j]), w_dw[j], row(b_dw[j]), row(ln_c_g[j]), row(ln_c_b[j]),
                            w_pw2, row(b_pw2[j]), w_ff1, w_ff2, fg)
    return xs.reshape(BATCH, SEQ, D)
```

```python
import functools

import jax
import jax.numpy as jnp
from jax import lax
from jax.experimental import pallas as pl
from jax.experimental.pallas import tpu as pltpu

D = 1024
BATCH = 4
SEQ = 4096
DEPTH = 4
GRID_W = 64
ROWS = SEQ // GRID_W
CTX_LEN = 256
CHUNK = 128
A_GROUPS = 4
A_W = 512
B_W = 512
HEAD_DIM = 64
HEADS = 8
WIN_H = 8
WIN_W = 16
Q_ROWS = 4
Q_BLK = Q_ROWS * GRID_W
K_ROWS = Q_ROWS + WIN_H
K_BLK = K_ROWS * GRID_W
ROW_STEPS = ROWS // Q_ROWS
MIX_IN = 2 * A_W + 3 * B_W
CONV_K = 31
HALO = 16
SUBLANES = 8
LANES = 128
CONV_ROWS = 128
FFN = 4 * D
FFN_CHUNK = 1024
FFN_CHUNKS = FFN // FFN_CHUNK
EPS = 1e-6
NEG = -1e30

T_LAT = BATCH * SEQ
T_CTX = BATCH * CTX_LEN
T_ALL = T_LAT + T_CTX
TILE = 256
IN_TILE = 1024
OUT_TILE = 512
CONV_PIECES = (D // LANES) * (TILE // CONV_ROWS)
TILES_PER_SEQ = SEQ // TILE
LAT_TILES = T_LAT // TILE
MOD_ROWS = 8
MOD_N = 1536
VMEM_LIMIT = 56 * 1024 * 1024

BF16 = jnp.bfloat16
F32 = jnp.float32


def _const_spec(shape):
    zeros = (0,) * len(shape)
    return pl.BlockSpec(shape, lambda *_: zeros, pipeline_mode=pl.Buffered(1))


def _layer_spec(stacked, idx):
    shape = stacked.shape[1:]
    zeros = (0,) * len(shape)
    return pl.BlockSpec((None,) + shape, lambda *_: (idx,) + zeros, pipeline_mode=pl.Buffered(1))


def _params(sem):
    return pltpu.CompilerParams(dimension_semantics=sem, vmem_limit_bytes=VMEM_LIMIT)


def _mod_row_spec(layer, tile_of=lambda i: i, tiles_per_seq=TILES_PER_SEQ):
    return pl.BlockSpec(
        (None, 1, 6 * D),
        lambda i: (layer * MOD_ROWS + jnp.minimum(tile_of(i) // tiles_per_seq, BATCH), 0, 0))


def _rms_mod(x, g, shift, scale):
    return x * lax.rsqrt(jnp.mean(x * x, axis=-1, keepdims=True) + EPS) * (g * (1.0 + scale)) + shift


def _dot(a, b):
    return jnp.dot(a, b, preferred_element_type=F32)


def _dot_nt(a, b):
    return lax.dot_general(a, b, (((1,), (1,)), ((), ())), preferred_element_type=F32)


def _mod_kernel(cc_ref, w_ref, b_ref, o_ref):
    cc = cc_ref[...]
    sc = (cc * jax.nn.sigmoid(cc)).astype(BF16)
    o_ref[...] = _dot(sc, w_ref[...].astype(BF16)) + b_ref[...]


def _modulation(cc, w_mod, b_mod):
    nblk = (6 * D) // MOD_N
    return pl.pallas_call(
        _mod_kernel,
        grid=(DEPTH, nblk),
        in_specs=[
            pl.BlockSpec((MOD_ROWS, D), lambda l, n: (0, 0)),
            pl.BlockSpec((None, D, MOD_N), lambda l, n: (l, 0, n)),
            pl.BlockSpec((None, 1, MOD_N), lambda l, n: (l, 0, n)),
        ],
        out_specs=pl.BlockSpec((None, MOD_ROWS, MOD_N), lambda l, n: (l, 0, n)),
        out_shape=jax.ShapeDtypeStruct((DEPTH, MOD_ROWS, 6 * D), F32),
        compiler_params=_params(("arbitrary", "arbitrary")),
    )(cc, w_mod, b_mod.reshape(DEPTH, 1, 6 * D))


def _load_stream(x_refs, lat_tiles):
    if len(x_refs) == 1:
        return x_refs[0][...]
    return jnp.where(pl.program_id(0) < lat_tiles, x_refs[0][...], x_refs[1][...])


def _stream_specs(xs, tile):
    if not isinstance(xs, tuple):
        return [pl.BlockSpec((tile, D), lambda i: (i, 0))], [xs]
    lat_tiles = xs[0].shape[0] // tile
    return [pl.BlockSpec((tile, D), lambda i: (jnp.minimum(i, lat_tiles - 1), 0)),
            pl.BlockSpec((tile, D), lambda i: (jnp.maximum(i - lat_tiles, 0), 0))], list(xs)


def _even_in_kernel(*refs, n_x):
    x_refs = refs[:n_x]
    mod_ref, ng_ref, w_in_ref, lng_ref, w_sp_ref, b_sp_ref, a_ref, q_ref, k_ref, v_ref = refs[n_x:]
    x = _load_stream(x_refs, T_LAT // IN_TILE)
    h = _rms_mod(x, ng_ref[...], mod_ref[:, 0:D], mod_ref[:, D:2 * D]).astype(BF16)
    z = jax.nn.gelu(_dot(h, w_in_ref[:, 0:2 * A_W]))
    gd = A_W // A_GROUPS
    for grp in range(A_GROUPS):
        g = z[:, A_W + grp * gd:A_W + (grp + 1) * gd]
        mu = jnp.mean(g, axis=-1, keepdims=True)
        gc = g - mu
        var = jnp.mean(gc * gc, axis=-1, keepdims=True)
        gn = (gc * lax.rsqrt(var + EPS) * lng_ref[:, grp * gd:(grp + 1) * gd]).astype(BF16)
        for ch in range(IN_TILE // CHUNK):
            rows = slice(ch * CHUNK, (ch + 1) * CHUNK)
            s = _dot(w_sp_ref[grp], gn[rows]) + b_sp_ref[grp]
            a_ref[rows, grp * gd:(grp + 1) * gd] = (z[rows, grp * gd:(grp + 1) * gd] * s).astype(BF16)
    c0 = 2 * A_W
    q_ref[...] = (_dot(h, w_in_ref[:, c0:c0 + B_W]) * (HEAD_DIM ** -0.5)).astype(BF16)
    k_ref[...] = _dot(h, w_in_ref[:, c0 + B_W:c0 + 2 * B_W]).astype(BF16)
    v_ref[...] = _dot(h, w_in_ref[:, c0 + 2 * B_W:c0 + 3 * B_W]).astype(BF16)


def _even_in(xs, mod3, layer, ng, w_in_all, j, lng, w_sp_all, b_sp):
    x_specs, x_args = _stream_specs(xs, IN_TILE)
    tok = lambda w: pl.BlockSpec((IN_TILE, w), lambda i: (i, 0))
    out = jax.ShapeDtypeStruct((T_ALL, A_W), BF16)
    return pl.pallas_call(
        functools.partial(_even_in_kernel, n_x=len(x_args)),
        grid=(T_ALL // IN_TILE,),
        in_specs=x_specs + [
            _mod_row_spec(layer, tiles_per_seq=SEQ // IN_TILE), _const_spec((1, D)), _layer_spec(w_in_all, j),
            _const_spec((1, A_W)), _layer_spec(w_sp_all, j), _const_spec((A_GROUPS, CHUNK, 1)),
        ],
        out_specs=[tok(A_W)] * 4,
        out_shape=[out] * 4,
        compiler_params=_params(("arbitrary",)),
    )(*x_args, mod3, ng, w_in_all, lng, w_sp_all, b_sp)


def _attn_kernel(q_ref, k_ref, v_ref, kc_ref, vc_ref, tbl_ref, o_ref, *, ctx_queries):
    r = pl.program_id(1)
    lane = lax.broadcasted_iota(jnp.int32, (Q_BLK, 2 * HEAD_DIM), 1)
    first_head = lane < HEAD_DIM

    def attend(local):
        if local:
            start = pl.multiple_of(
                jnp.clip(r * Q_ROWS - WIN_H // 2, 0, ROWS - K_ROWS) * GRID_W, GRID_W)
        for hp in range(HEADS // 2):
            cols = slice(hp * 2 * HEAD_DIM, (hp + 1) * 2 * HEAD_DIM)
            q2 = q_ref[:, cols]
            k2 = kc_ref[:, cols]
            v2 = vc_ref[:, cols]
            if local:
                k2 = jnp.concatenate([k_ref[pl.ds(start, K_BLK), cols], k2], axis=0)
                v2 = jnp.concatenate([v_ref[pl.ds(start, K_BLK), cols], v2], axis=0)
            outs = []
            for half in range(2):
                qm = jnp.where(first_head if half == 0 else ~first_head, q2, jnp.zeros_like(q2))
                s = _dot_nt(qm, k2)
                if local:
                    s = jnp.concatenate([s[:, 0:K_BLK] + tbl_ref[2 * hp + half], s[:, K_BLK:]], axis=1)
                m = jnp.max(s, axis=-1, keepdims=True)
                p = jnp.exp(s - m)
                den = jnp.sum(p, axis=-1, keepdims=True)
                outs.append(_dot(p.astype(BF16), v2) / den)
            o_ref[:, cols] = jnp.where(first_head, outs[0], outs[1]).astype(BF16)

    if ctx_queries:
        pl.when(r < ROW_STEPS)(lambda: attend(True))
        pl.when(r >= ROW_STEPS)(lambda: attend(False))
    else:
        attend(True)


def _attention(q, k, v, tbl, ctx_queries):
    steps = ROW_STEPS + (1 if ctx_queries else 0)
    lat_blocks = T_LAT // Q_BLK

    def q_idx(b, r):
        return (jnp.where(r < ROW_STEPS, b * ROW_STEPS + r, lat_blocks + b), 0)

    def tbl_idx(b, r):
        return (jnp.where(r == 0, 0, jnp.where(r >= ROW_STEPS - 1, 2, 1)), 0, 0, 0)

    ctx_blk = T_LAT // CTX_LEN
    return pl.pallas_call(
        functools.partial(_attn_kernel, ctx_queries=ctx_queries),
        grid=(BATCH, steps),
        in_specs=[
            pl.BlockSpec((Q_BLK, B_W), q_idx),
            pl.BlockSpec((SEQ, B_W), lambda b, r: (b, 0)),
            pl.BlockSpec((SEQ, B_W), lambda b, r: (b, 0)),
            pl.BlockSpec((CTX_LEN, B_W), lambda b, r: (ctx_blk + b, 0)),
            pl.BlockSpec((CTX_LEN, B_W), lambda b, r: (ctx_blk + b, 0)),
            pl.BlockSpec((None, HEADS, Q_BLK, K_BLK), tbl_idx),
        ],
        out_specs=pl.BlockSpec((Q_BLK, B_W), q_idx),
        out_shape=jax.ShapeDtypeStruct((T_ALL if ctx_queries else T_LAT, B_W), BF16),
        compiler_params=_params(("arbitrary", "arbitrary")),
    )(q, k, v, k, v, tbl)


N_DR = 2 * WIN_H - 1
N_DC = 2 * WIN_W - 1
TBL_R0 = (0, Q_ROWS, ROWS - Q_ROWS)


def _tbl_kernel(rpb_ref, o_ref, t_scr):
    h = pl.program_id(0)
    shape = (GRID_W, 2 * GRID_W)
    cq = lax.broadcasted_iota(jnp.int32, shape, 0)
    lane = lax.broadcasted_iota(jnp.int32, shape, 1)
    ck = lane & (GRID_W - 1)
    dc = jnp.clip(ck - cq + WIN_W - 1, 0, N_DC - 1)
    cs = jnp.clip(cq - WIN_W // 2, 0, GRID_W - WIN_W)
    col_ok = (ck >= cs) & (ck < cs + WIN_W)
    neg = jnp.full(shape, NEG, F32)
    for dr in range(N_DR):
        t = neg
        for j in range(N_DC):
            t = jnp.where(dc == j, rpb_ref[(h * N_DR + dr) * N_DC + j], t)
        t_scr[dr] = jnp.where(col_ok, t, NEG)
    left = lane < GRID_W
    for v, r0 in enumerate(TBL_R0):
        k0 = min(max(r0 - WIN_H // 2, 0), ROWS - K_ROWS)
        for qi in range(Q_ROWS):
            qrow = r0 + qi
            w0 = min(max(qrow - WIN_H // 2, 0), ROWS - WIN_H)
            for kp in range(K_ROWS // 2):
                halves = []
                for krow in (k0 + 2 * kp, k0 + 2 * kp + 1):
                    in_win = w0 <= krow < w0 + WIN_H
                    halves.append(t_scr[krow - qrow + WIN_H - 1] if in_win else neg)
                o_ref[v, qi * GRID_W:(qi + 1) * GRID_W, kp * 2 * GRID_W:(kp + 1) * 2 * GRID_W] = (
                    jnp.where(left, halves[0], halves[1]))


def _bias_tables(rpb):
    return pl.pallas_call(
        _tbl_kernel,
        grid=(HEADS,),
        in_specs=[pl.BlockSpec(memory_space=pltpu.SMEM)],
        out_specs=pl.BlockSpec((len(TBL_R0), None, Q_BLK, K_BLK), lambda h: (0, h, 0, 0)),
        out_shape=jax.ShapeDtypeStruct((len(TBL_R0), HEADS, Q_BLK, K_BLK), F32),
        scratch_shapes=[pltpu.VMEM((N_DR, GRID_W, 2 * GRID_W), F32)],
        compiler_params=_params(("arbitrary",)),
    )(rpb.reshape(-1))


def _pin_after(value, anchor):
    bits = lambda t: pltpu.bitcast(t, jnp.uint32)
    zero = lax.shift_right_logical(lax.shift_right_logical(bits(anchor), jnp.uint32(16)), jnp.uint32(16))
    return pltpu.bitcast(bits(value) | zero, F32)


def _ffn_tail(x, y, mod_ref, ng2_ref, w1_ref, w2_ref, fg_ref, side_work=None, h2_scr=None):
    x1 = x + mod_ref[:, 2 * D:3 * D] * y
    h2f = _rms_mod(x1, ng2_ref[...], mod_ref[:, 3 * D:4 * D], mod_ref[:, 4 * D:5 * D])
    h2 = h2f.astype(BF16)
    if side_work is not None:
        h2_scr[...] = h2
    acc = None
    token = None
    for c in range(FFN_CHUNKS):
        if side_work is not None:
            if token is not None:
                h2_scr[0:2 * SUBLANES, 0:LANES] = _pin_after(h2f[0:2 * SUBLANES, 0:LANES], token).astype(BF16)
                h2 = h2_scr[...]
            token = side_work(c, y if c == 0 else acc)
        cs = slice(c * FFN_CHUNK, (c + 1) * FFN_CHUNK)
        hid = jnp.maximum(_dot(h2, w1_ref[:, cs]), 0.0)
        part = _dot((hid * hid).astype(BF16), w2_ref[cs, :])
        acc = part if acc is None else acc + part
    x2 = x1 + mod_ref[:, 5 * D:6 * D] * acc
    if fg_ref is not None:
        x2 = x2 * lax.rsqrt(jnp.mean(x2 * x2, axis=-1, keepdims=True) + EPS) * fg_ref[...]
    return x2


def _even_out_kernel(*refs, final, n_x):
    x_refs, refs = refs[:n_x], refs[n_x:]
    if final:
        a_ref, b_ref, mod_ref, ng2_ref, wo_ref, w1_ref, w2_ref, fg_ref, o_ref = refs
    else:
        a_ref, b_ref, mod_ref, ng2_ref, wo_ref, w1_ref, w2_ref, o_ref = refs
        fg_ref = None
    y = _dot(a_ref[...], wo_ref[0:A_W, :]) + _dot(b_ref[...], wo_ref[A_W:A_W + B_W, :])
    o_ref[...] = _ffn_tail(_load_stream(x_refs, T_LAT // OUT_TILE), y, mod_ref, ng2_ref, w1_ref, w2_ref, fg_ref)


def _even_out(a, b, xs, mod3, layer, ng2, w_out_all, j, w1_all, w2_all, n_out, final_g=None):
    x_specs, x_args = _stream_specs(xs, OUT_TILE)
    tok = lambda w: pl.BlockSpec((OUT_TILE, w), lambda i: (i, 0))
    in_specs = x_specs + [tok(A_W), tok(B_W), _mod_row_spec(layer, tiles_per_seq=SEQ // OUT_TILE),
                          _const_spec((1, D)),
                          _layer_spec(w_out_all, j), _layer_spec(w1_all, layer), _layer_spec(w2_all, layer)]
    args = x_args + [a, b, mod3, ng2, w_out_all, w1_all, w2_all]
    if final_g is not None:
        in_specs.append(_const_spec((1, D)))
        args.append(final_g)
    return pl.pallas_call(
        functools.partial(_even_out_kernel, final=final_g is not None, n_x=len(x_args)),
        grid=(n_out // OUT_TILE,),
        in_specs=in_specs,
        out_specs=tok(D),
        out_shape=jax.ShapeDtypeStruct((n_out, D), F32),
        compiler_params=_params(("arbitrary",)),
    )(*args)


def _odd_kernel(*refs, final, n_tiles):
    if final:
        (xp_ref, xa_ref, xn_ref, xb_ref, moda_ref, modb_ref, ng1_ref, ng2_ref, wp1_ref, bp1_ref, wdw_ref,
         bdw_ref, lg_ref, lb_ref, wp2_ref, bp2_ref, w1_ref, w2_ref, fg_ref, o_ref, y_scr, conv_scr, w_scr, h2_scr) = refs
    else:
        (xp_ref, xa_ref, xn_ref, xb_ref, moda_ref, modb_ref, ng1_ref, ng2_ref, wp1_ref, bp1_ref, wdw_ref,
         bdw_ref, lg_ref, lb_ref, wp2_ref, bp2_ref, w1_ref, w2_ref, o_ref, y_scr, conv_scr, w_scr, h2_scr) = refs
        fg_ref = None
    i = pl.program_id(0)

    @pl.when(i == 0)
    def _():
        conv_scr[...] = jnp.zeros_like(conv_scr)

    ia = jnp.minimum(i, n_tiles - 1)
    is_lat = ia < LAT_TILES
    pos = ia % TILES_PER_SEQ
    seq_first = jnp.logical_or(jnp.logical_not(is_lat), pos == 0)
    seq_last = jnp.logical_or(jnp.logical_not(is_lat), pos == TILES_PER_SEQ - 1)
    xe = jnp.concatenate([xp_ref[...], xa_ref[...], xn_ref[...]], axis=0)
    h = _rms_mod(xe, ng1_ref[...], moda_ref[:, 0:D], moda_ref[:, D:2 * D]).astype(BF16)
    t = _dot(h, wp1_ref[...]) + bp1_ref[...]
    glu = t[:, 0:D] * jax.nn.sigmoid(t[:, D:2 * D])
    for cb in range(D // LANES):
        lanes = slice(cb * LANES, (cb + 1) * LANES)
        y_scr[cb, 0:HALO] = jnp.where(seq_first, 0.0, glu[0:HALO, lanes])
        y_scr[cb, HALO:HALO + TILE] = glu[HALO:HALO + TILE, lanes]
        y_scr[cb, HALO + TILE:] = jnp.where(seq_last, 0.0, glu[HALO + TILE:, lanes])

    acc = jnp.concatenate([conv_scr[cb] for cb in range(D // LANES)], axis=1)
    mu = jnp.mean(acc, axis=-1, keepdims=True)
    ac = acc - mu
    var = jnp.mean(ac * ac, axis=-1, keepdims=True)
    ln = ac * lax.rsqrt(var + EPS) * lg_ref[...] + lb_ref[...]
    act = (ln * jax.nn.sigmoid(ln)).astype(BF16)
    y = _dot(act, wp2_ref[...]) + bp2_ref[...]

    base = HALO - CONV_K // 2
    rows_in = CONV_ROWS + SUBLANES

    row_blocks = TILE // CONV_ROWS

    def conv_piece(p, anchor):
        cb, r0 = p // row_blocks, (p % row_blocks) * CONV_ROWS
        lanes = slice(cb * LANES, (cb + 1) * LANES)
        w_scr[p, 0:CONV_K, :] = _pin_after(wdw_ref[:, lanes], anchor[0:1, 0:LANES])
        out = bdw_ref[:, lanes]
        for s in range(SUBLANES):
            part = None
            for off in range(s, base + CONV_K, SUBLANES):
                if off < base:
                    continue
                r1 = r0 + off - s
                term = y_scr[cb, r1:r1 + rows_in, :] * w_scr[p, off - base:off - base + 1, :]
                part = term if part is None else part + term
            out = out + part[s:s + CONV_ROWS]
        conv_scr[cb, r0:r0 + CONV_ROWS, :] = out
        return out[0:1, :]

    per_slot = CONV_PIECES // FFN_CHUNKS

    def conv_part(slot, anchor):
        for p in range(slot * per_slot, (slot + 1) * per_slot):
            token = conv_piece(p, anchor)
        return token

    o_ref[...] = _ffn_tail(xb_ref[...], y, modb_ref, ng2_ref, w1_ref, w2_ref, fg_ref,
                           side_work=conv_part, h2_scr=h2_scr)


def _odd_layer(xs, mod3, layer, ng1, ng2, wp1_all, j, bp1, wdw, bdw, lg, lb, wp2_all, bp2, w1_all, w2_all,
               final_g=None):
    n = xs.shape[0]
    n_tiles = n // TILE
    per = TILE // HALO
    halo_blocks = n // HALO
    tile_a = lambda i: jnp.minimum(i, n_tiles - 1)
    tile_b = lambda i: jnp.maximum(i - 1, 0)
    in_specs = [
        pl.BlockSpec((HALO, D), lambda i: (jnp.maximum(tile_a(i) * per - 1, 0), 0)),
        pl.BlockSpec((TILE, D), lambda i: (tile_a(i), 0)),
        pl.BlockSpec((HALO, D), lambda i: (jnp.minimum((tile_a(i) + 1) * per, halo_blocks - 1), 0)),
        pl.BlockSpec((TILE, D), lambda i: (tile_b(i), 0)),
        _mod_row_spec(layer, tile_a), _mod_row_spec(layer, tile_b), _const_spec((1, D)), _const_spec((1, D)),
        _layer_spec(wp1_all, j), _const_spec((1, 2 * D)), _const_spec((CONV_K, D)), _const_spec((1, D)),
        _const_spec((1, D)), _const_spec((1, D)), _layer_spec(wp2_all, j), _const_spec((1, D)),
        _layer_spec(w1_all, layer), _layer_spec(w2_all, layer),
    ]
    args = [xs, xs, xs, xs, mod3, mod3, ng1, ng2, wp1_all, bp1, wdw, bdw, lg, lb, wp2_all, bp2, w1_all, w2_all]
    if final_g is not None:
        in_specs.append(_const_spec((1, D)))
        args.append(final_g)
    return pl.pallas_call(
        functools.partial(_odd_kernel, final=final_g is not None, n_tiles=n_tiles),
        grid=(n_tiles + 1,),
        in_specs=in_specs,
        out_specs=pl.BlockSpec((TILE, D), lambda i: (tile_b(i), 0)),
        out_shape=jax.ShapeDtypeStruct((n, D), F32),
        scratch_shapes=[pltpu.VMEM((D // LANES, HALO + TILE + HALO, LANES), F32),
                        pltpu.VMEM((D // LANES, TILE, LANES), F32),
                        pltpu.VMEM((CONV_PIECES, CONV_K + 1, LANES), F32), pltpu.VMEM((TILE, D), BF16)],
        compiler_params=_params(("arbitrary",)),
    )(*args)


def kernel(x, c, ctx, c_ctx, w_mod, b_mod, norm_g, w_in, w_out, ln_v_g, w_sp, b_sp, rpb, w_pw1, b_pw1,
           w_dw, b_dw, ln_c_g, ln_c_b, w_pw2, b_pw2, w_ff1, w_ff2, final_g):
    row = lambda t: t.reshape(1, -1)
    cc = jnp.concatenate([c, c_ctx[None, :], jnp.zeros((MOD_ROWS - BATCH - 1, D), F32)], axis=0)
    mod3 = _modulation(cc, w_mod, b_mod).reshape(DEPTH * MOD_ROWS, 1, 6 * D)
    xs = (x.reshape(T_LAT, D), ctx.reshape(T_CTX, D))
    w_in, w_out, w_sp, w_pw1, w_pw2, w_ff1, w_ff2 = (
        w.astype(BF16) for w in (w_in, w_out, w_sp, w_pw1, w_pw2, w_ff1, w_ff2))
    last_reader = ((DEPTH - 1) // 2) * 2
    for l in range(DEPTH):
        ctx_out = l < last_reader
        fg = row(final_g) if l == DEPTH - 1 else None
        j = l // 2
        if l % 2 == 0:
            a, q, k, v = _even_in(xs, mod3, l, row(norm_g[l, 0]), w_in, j, row(ln_v_g[j]),
                                  w_sp, b_sp[j][:, :, None])
            b = _attention(q, k, v, _bias_tables(rpb[j]), ctx_out)
            xs = _even_out(a, b, xs, mod3, l, row(norm_g[l, 1]), w_out, j, w_ff1, w_ff2,
                           T_ALL if ctx_out else T_LAT, fg)
        else:
            xs = _odd_layer(xs, mod3, l, row(norm_g[l, 0]), row(norm_g[l, 1]), w_pw1, j,
                            row(b_pw1[j]), w_dw[j], row(b_dw[j]), row(ln_c_g[j]), row(ln_c_b[j]),
                            w_pw2, row(b_pw2[j]), w_ff1, w_ff2, fg)
    return xs.reshape(BATCH, SEQ, D)
```

```python
import functools

import jax
import jax.numpy as jnp
from jax import lax
from jax.experimental import pallas as pl
from jax.experimental.pallas import tpu as pltpu

D = 1024
BATCH = 4
SEQ = 4096
DEPTH = 4
GRID_W = 64
ROWS = SEQ // GRID_W
CTX_LEN = 256
CHUNK = 128
A_GROUPS = 4
A_W = 512
B_W = 512
HEAD_DIM = 64
HEADS = 8
WIN_H = 8
WIN_W = 16
Q_ROWS = 4
Q_BLK = Q_ROWS * GRID_W
K_ROWS = Q_ROWS + WIN_H
K_BLK = K_ROWS * GRID_W
ROW_STEPS = ROWS // Q_ROWS
MIX_IN = 2 * A_W + 3 * B_W
CONV_K = 31
HALO = 16
SUBLANES = 8
LANES = 128
CONV_ROWS = 128
FFN = 4 * D
FFN_CHUNK = 1024
FFN_CHUNKS = FFN // FFN_CHUNK
EPS = 1e-6
NEG = -1e30

T_LAT = BATCH * SEQ
T_CTX = BATCH * CTX_LEN
T_ALL = T_LAT + T_CTX
TILE = 256
IN_TILE = 1024
OUT_TILE = 512
CONV_PIECES = (D // LANES) * (TILE // CONV_ROWS)
TILES_PER_SEQ = SEQ // TILE
LAT_TILES = T_LAT // TILE
MOD_ROWS = 8
MOD_N = 1536
VMEM_LIMIT = 56 * 1024 * 1024

BF16 = jnp.bfloat16
F32 = jnp.float32


def _const_spec(shape):
    zeros = (0,) * len(shape)
    return pl.BlockSpec(shape, lambda *_: zeros, pipeline_mode=pl.Buffered(1))


def _layer_spec(stacked, idx):
    shape = stacked.shape[1:]
    zeros = (0,) * len(shape)
    return pl.BlockSpec((None,) + shape, lambda *_: (idx,) + zeros, pipeline_mode=pl.Buffered(1))


def _params(sem):
    return pltpu.CompilerParams(dimension_semantics=sem, vmem_limit_bytes=VMEM_LIMIT)


def _mod_row_spec(layer, tile_of=lambda i: i, tiles_per_seq=TILES_PER_SEQ):
    return pl.BlockSpec(
        (None, 1, 6 * D),
        lambda i: (layer * MOD_ROWS + jnp.minimum(tile_of(i) // tiles_per_seq, BATCH), 0, 0))


def _rms_mod(x, g, shift, scale):
    return x * lax.rsqrt(jnp.mean(x * x, axis=-1, keepdims=True) + EPS) * (g * (1.0 + scale)) + shift


def _dot(a, b):
    return jnp.dot(a, b, preferred_element_type=F32)


def _dot_nt(a, b):
    return lax.dot_general(a, b, (((1,), (1,)), ((), ())), preferred_element_type=F32)


def _mod_kernel(cc_ref, w_ref, b_ref, o_ref):
    cc = cc_ref[...]
    sc = (cc * jax.nn.sigmoid(cc)).astype(BF16)
    o_ref[...] = _dot(sc, w_ref[...].astype(BF16)) + b_ref[...]


def _modulation(cc, w_mod, b_mod):
    nblk = (6 * D) // MOD_N
    return pl.pallas_call(
        _mod_kernel,
        grid=(DEPTH, nblk),
        in_specs=[
            pl.BlockSpec((MOD_ROWS, D), lambda l, n: (0, 0)),
            pl.BlockSpec((None, D, MOD_N), lambda l, n: (l, 0, n)),
            pl.BlockSpec((None, 1, MOD_N), lambda l, n: (l, 0, n)),
        ],
        out_specs=pl.BlockSpec((None, MOD_ROWS, MOD_N), lambda l, n: (l, 0, n)),
        out_shape=jax.ShapeDtypeStruct((DEPTH, MOD_ROWS, 6 * D), F32),
        compiler_params=_params(("arbitrary", "arbitrary")),
    )(cc, w_mod, b_mod.reshape(DEPTH, 1, 6 * D))


def _load_stream(x_refs, lat_tiles):
    if len(x_refs) == 1:
        return x_refs[0][...]
    return jnp.where(pl.program_id(0) < lat_tiles, x_refs[0][...], x_refs[1][...])


def _stream_specs(xs, tile):
    if not isinstance(xs, tuple):
        return [pl.BlockSpec((tile, D), lambda i: (i, 0))], [xs]
    lat_tiles = xs[0].shape[0] // tile
    return [pl.BlockSpec((tile, D), lambda i: (jnp.minimum(i, lat_tiles - 1), 0)),
            pl.BlockSpec((tile, D), lambda i: (jnp.maximum(i - lat_tiles, 0), 0))], list(xs)


def _even_in_kernel(*refs, n_x):
    x_refs = refs[:n_x]
    mod_ref, ng_ref, w_in_ref, lng_ref, w_sp_ref, b_sp_ref, a_ref, q_ref, k_ref, v_ref = refs[n_x:]
    x = _load_stream(x_refs, T_LAT // IN_TILE)
    h = _rms_mod(x, ng_ref[...], mod_ref[:, 0:D], mod_ref[:, D:2 * D]).astype(BF16)
    z = jax.nn.gelu(_dot(h, w_in_ref[:, 0:2 * A_W]))
    gd = A_W // A_GROUPS
    for grp in range(A_GROUPS):
        g = z[:, A_W + grp * gd:A_W + (grp + 1) * gd]
        mu = jnp.mean(g, axis=-1, keepdims=True)
        gc = g - mu
        var = jnp.mean(gc * gc, axis=-1, keepdims=True)
        gn = (gc * lax.rsqrt(var + EPS) * lng_ref[:, grp * gd:(grp + 1) * gd]).astype(BF16)
        for ch in range(IN_TILE // CHUNK):
            rows = slice(ch * CHUNK, (ch + 1) * CHUNK)
            s = _dot(w_sp_ref[grp], gn[rows]) + b_sp_ref[grp]
            a_ref[rows, grp * gd:(grp + 1) * gd] = (z[rows, grp * gd:(grp + 1) * gd] * s).astype(BF16)
    c0 = 2 * A_W
    q_ref[...] = (_dot(h, w_in_ref[:, c0:c0 + B_W]) * (HEAD_DIM ** -0.5)).astype(BF16)
    k_ref[...] = _dot(h, w_in_ref[:, c0 + B_W:c0 + 2 * B_W]).astype(BF16)
    v_ref[...] = _dot(h, w_in_ref[:, c0 + 2 * B_W:c0 + 3 * B_W]).astype(BF16)


def _even_in(xs, mod3, layer, ng, w_in_all, j, lng, w_sp_all, b_sp):
    x_specs, x_args = _stream_specs(xs, IN_TILE)
    tok = lambda w: pl.BlockSpec((IN_TILE, w), lambda i: (i, 0))
    out = jax.ShapeDtypeStruct((T_ALL, A_W), BF16)
    return pl.pallas_call(
        functools.partial(_even_in_kernel, n_x=len(x_args)),
        grid=(T_ALL // IN_TILE,),
        in_specs=x_specs + [
            _mod_row_spec(layer, tiles_per_seq=SEQ // IN_TILE), _const_spec((1, D)), _layer_spec(w_in_all, j),
            _const_spec((1, A_W)), _layer_spec(w_sp_all, j), _const_spec((A_GROUPS, CHUNK, 1)),
        ],
        out_specs=[tok(A_W)] * 4,
        out_shape=[out] * 4,
        compiler_params=_params(("arbitrary",)),
    )(*x_args, mod3, ng, w_in_all, lng, w_sp_all, b_sp)


def _attn_kernel(q_ref, k_ref, v_ref, kc_ref, vc_ref, tbl_ref, o_ref, *, ctx_queries):
    r = pl.program_id(1)
    lane = lax.broadcasted_iota(jnp.int32, (Q_BLK, 2 * HEAD_DIM), 1)
    first_head = lane < HEAD_DIM

    def attend(local):
        if local:
            start = pl.multiple_of(
                jnp.clip(r * Q_ROWS - WIN_H // 2, 0, ROWS - K_ROWS) * GRID_W, GRID_W)
        for hp in range(HEADS // 2):
            cols = slice(hp * 2 * HEAD_DIM, (hp + 1) * 2 * HEAD_DIM)
            q2 = q_ref[:, cols]
            k2 = kc_ref[:, cols]
            v2 = vc_ref[:, cols]
            if local:
                k2 = jnp.concatenate([k_ref[pl.ds(start, K_BLK), cols], k2], axis=0)
                v2 = jnp.concatenate([v_ref[pl.ds(start, K_BLK), cols], v2], axis=0)
            outs = []
            for half in range(2):
                qm = jnp.where(first_head if half == 0 else ~first_head, q2, jnp.zeros_like(q2))
                s = _dot_nt(qm, k2)
                if local:
                    s = jnp.concatenate([s[:, 0:K_BLK] + tbl_ref[2 * hp + half], s[:, K_BLK:]], axis=1)
                m = jnp.max(s, axis=-1, keepdims=True)
                p = jnp.exp(s - m)
                den = jnp.sum(p, axis=-1, keepdims=True)
                outs.append(_dot(p.astype(BF16), v2) / den)
            o_ref[:, cols] = jnp.where(first_head, outs[0], outs[1]).astype(BF16)

    if ctx_queries:
        pl.when(r < ROW_STEPS)(lambda: attend(True))
        pl.when(r >= ROW_STEPS)(lambda: attend(False))
    else:
        attend(True)


def _attention(q, k, v, tbl, ctx_queries):
    steps = ROW_STEPS + (1 if ctx_queries else 0)
    lat_blocks = T_LAT // Q_BLK

    def q_idx(b, r):
        return (jnp.where(r < ROW_STEPS, b * ROW_STEPS + r, lat_blocks + b), 0)

    def tbl_idx(b, r):
        return (jnp.where(r == 0, 0, jnp.where(r >= ROW_STEPS - 1, 2, 1)), 0, 0, 0)

    ctx_blk = T_LAT // CTX_LEN
    return pl.pallas_call(
        functools.partial(_attn_kernel, ctx_queries=ctx_queries),
        grid=(BATCH, steps),
        in_specs=[
            pl.BlockSpec((Q_BLK, B_W), q_idx),
            pl.BlockSpec((SEQ, B_W), lambda b, r: (b, 0)),
            pl.BlockSpec((SEQ, B_W), lambda b, r: (b, 0)),
            pl.BlockSpec((CTX_LEN, B_W), lambda b, r: (ctx_blk + b, 0)),
            pl.BlockSpec((CTX_LEN, B_W), lambda b, r: (ctx_blk + b, 0)),
            pl.BlockSpec((None, HEADS, Q_BLK, K_BLK), tbl_idx),
        ],
        out_specs=pl.BlockSpec((Q_BLK, B_W), q_idx),
        out_shape=jax.ShapeDtypeStruct((T_ALL if ctx_queries else T_LAT, B_W), BF16),
        compiler_params=_params(("arbitrary", "arbitrary")),
    )(q, k, v, k, v, tbl)


N_DR = 2 * WIN_H - 1
N_DC = 2 * WIN_W - 1
TBL_R0 = (0, Q_ROWS, ROWS - Q_ROWS)


def _tbl_kernel(rpb_ref, o_ref, t_scr):
    h = pl.program_id(0)
    shape = (GRID_W, 2 * GRID_W)
    cq = lax.broadcasted_iota(jnp.int32, shape, 0)
    lane = lax.broadcasted_iota(jnp.int32, shape, 1)
    ck = lane & (GRID_W - 1)
    dc = jnp.clip(ck - cq + WIN_W - 1, 0, N_DC - 1)
    cs = jnp.clip(cq - WIN_W // 2, 0, GRID_W - WIN_W)
    col_ok = (ck >= cs) & (ck < cs + WIN_W)
    neg = jnp.full(shape, NEG, F32)
    for dr in range(N_DR):
        t = neg
        for j in range(N_DC):
            t = jnp.where(dc == j, rpb_ref[(h * N_DR + dr) * N_DC + j], t)
        t_scr[dr] = jnp.where(col_ok, t, NEG)
    left = lane < GRID_W
    for v, r0 in enumerate(TBL_R0):
        k0 = min(max(r0 - WIN_H // 2, 0), ROWS - K_ROWS)
        for qi in range(Q_ROWS):
            qrow = r0 + qi
            w0 = min(max(qrow - WIN_H // 2, 0), ROWS - WIN_H)
            for kp in range(K_ROWS // 2):
                halves = []
                for krow in (k0 + 2 * kp, k0 + 2 * kp + 1):
                    in_win = w0 <= krow < w0 + WIN_H
                    halves.append(t_scr[krow - qrow + WIN_H - 1] if in_win else neg)
                o_ref[v, qi * GRID_W:(qi + 1) * GRID_W, kp * 2 * GRID_W:(kp + 1) * 2 * GRID_W] = (
                    jnp.where(left, halves[0], halves[1]))


def _bias_tables(rpb):
    return pl.pallas_call(
        _tbl_kernel,
        grid=(HEADS,),
        in_specs=[pl.BlockSpec(memory_space=pltpu.SMEM)],
        out_specs=pl.BlockSpec((len(TBL_R0), None, Q_BLK, K_BLK), lambda h: (0, h, 0, 0)),
        out_shape=jax.ShapeDtypeStruct((len(TBL_R0), HEADS, Q_BLK, K_BLK), F32),
        scratch_shapes=[pltpu.VMEM((N_DR, GRID_W, 2 * GRID_W), F32)],
        compiler_params=_params(("arbitrary",)),
    )(rpb.reshape(-1))


def _pin_after(value, anchor):
    bits = lambda t: pltpu.bitcast(t, jnp.uint32)
    zero = lax.shift_right_logical(lax.shift_right_logical(bits(anchor), jnp.uint32(16)), jnp.uint32(16))
    return pltpu.bitcast(bits(value) | zero, F32)


def _ffn_tail(x, y, mod_ref, ng2_ref, w1_ref, w2_ref, fg_ref, side_work=None, h2_scr=None):
    x1 = x + mod_ref[:, 2 * D:3 * D] * y
    h2f = _rms_mod(x1, ng2_ref[...], mod_ref[:, 3 * D:4 * D], mod_ref[:, 4 * D:5 * D])
    h2 = h2f.astype(BF16)
    if side_work is not None:
        h2_scr[...] = h2
    acc = None
    token = None
    for c in range(FFN_CHUNKS):
        if side_work is not None:
            if token is not None:
                h2_scr[0:2 * SUBLANES, 0:LANES] = _pin_after(h2f[0:2 * SUBLANES, 0:LANES], token).astype(BF16)
                h2 = h2_scr[...]
            token = side_work(c, y if c == 0 else acc)
        cs = slice(c * FFN_CHUNK, (c + 1) * FFN_CHUNK)
        hid = jnp.maximum(_dot(h2, w1_ref[:, cs]), 0.0)
        part = _dot((hid * hid).astype(BF16), w2_ref[cs, :])
        acc = part if acc is None else acc + part
    x2 = x1 + mod_ref[:, 5 * D:6 * D] * acc
    if fg_ref is not None:
        x2 = x2 * lax.rsqrt(jnp.mean(x2 * x2, axis=-1, keepdims=True) + EPS) * fg_ref[...]
    return x2


IN_HBM = pl.BlockSpec(memory_space=pl.ANY)


def _fetch_weights(pairs, sem):
    copies = [pltpu.make_async_copy(src, dst, sem.at[n]) for n, (src, dst) in enumerate(pairs)]
    for cp in copies:
        cp.start()
    for cp in copies:
        cp.wait()


def _even_out_kernel(*refs, final, n_x, layer, j):
    x_refs, refs = refs[:n_x], refs[n_x:]
    a_ref, b_ref, mod_ref, ng2_ref, wo_hbm, w1_hbm, w2_hbm = refs[:7]
    fg_ref = refs[7] if final else None
    o_ref, wo_ref, w1_ref, w2_ref, w_sem = refs[-5:]

    @pl.when(pl.program_id(0) == 0)
    def _():
        _fetch_weights(((wo_hbm.at[j], wo_ref), (w1_hbm.at[layer], w1_ref), (w2_hbm.at[layer], w2_ref)), w_sem)

    y = _dot(a_ref[...], wo_ref[0:A_W, :]) + _dot(b_ref[...], wo_ref[A_W:A_W + B_W, :])
    o_ref[...] = _ffn_tail(_load_stream(x_refs, T_LAT // OUT_TILE), y, mod_ref, ng2_ref, w1_ref, w2_ref, fg_ref)


def _even_out(a, b, xs, mod3, layer, ng2, w_out_all, j, w1_all, w2_all, n_out, final_g=None):
    x_specs, x_args = _stream_specs(xs, OUT_TILE)
    tok = lambda w: pl.BlockSpec((OUT_TILE, w), lambda i: (i, 0))
    in_specs = x_specs + [tok(A_W), tok(B_W), _mod_row_spec(layer, tiles_per_seq=SEQ // OUT_TILE),
                          _const_spec((1, D)), IN_HBM, IN_HBM, IN_HBM]
    args = x_args + [a, b, mod3, ng2, w_out_all, w1_all, w2_all]
    if final_g is not None:
        in_specs.append(_const_spec((1, D)))
        args.append(final_g)
    return pl.pallas_call(
        functools.partial(_even_out_kernel, final=final_g is not None, n_x=len(x_args), layer=layer, j=j),
        grid=(n_out // OUT_TILE,),
        in_specs=in_specs,
        out_specs=tok(D),
        out_shape=jax.ShapeDtypeStruct((n_out, D), F32),
        scratch_shapes=[pltpu.VMEM((A_W + B_W, D), BF16), pltpu.VMEM((D, FFN), BF16), pltpu.VMEM((FFN, D), BF16),
                        pltpu.SemaphoreType.DMA((3,))],
        compiler_params=_params(("arbitrary",)),
    )(*args)


def _odd_kernel(*refs, final, n_tiles, layer, j):
    (xp_ref, xa_ref, xn_ref, xb_ref, moda_ref, modb_ref, ng1_ref, ng2_ref, wp1_hbm, bp1_ref, wdw_ref,
     bdw_ref, lg_ref, lb_ref, wp2_hbm, bp2_ref, w1_hbm, w2_hbm) = refs[:18]
    fg_ref = refs[18] if final else None
    o_ref, y_scr, conv_scr, w_scr, h2_scr, wp1_ref, wp2_ref, w1_ref, w2_ref, w_sem = refs[-10:]
    i = pl.program_id(0)

    @pl.when(i == 0)
    def _():
        conv_scr[...] = jnp.zeros_like(conv_scr)
        _fetch_weights(((wp1_hbm.at[j], wp1_ref), (wp2_hbm.at[j], wp2_ref),
                        (w1_hbm.at[layer], w1_ref), (w2_hbm.at[layer], w2_ref)), w_sem)

    ia = jnp.minimum(i, n_tiles - 1)
    is_lat = ia < LAT_TILES
    pos = ia % TILES_PER_SEQ
    seq_first = jnp.logical_or(jnp.logical_not(is_lat), pos == 0)
    seq_last = jnp.logical_or(jnp.logical_not(is_lat), pos == TILES_PER_SEQ - 1)
    xe = jnp.concatenate([xp_ref[...], xa_ref[...], xn_ref[...]], axis=0)
    h = _rms_mod(xe, ng1_ref[...], moda_ref[:, 0:D], moda_ref[:, D:2 * D]).astype(BF16)
    t = _dot(h, wp1_ref[...]) + bp1_ref[...]
    glu = t[:, 0:D] * jax.nn.sigmoid(t[:, D:2 * D])
    for cb in range(D // LANES):
        lanes = slice(cb * LANES, (cb + 1) * LANES)
        y_scr[cb, 0:HALO] = jnp.where(seq_first, 0.0, glu[0:HALO, lanes])
        y_scr[cb, HALO:HALO + TILE] = glu[HALO:HALO + TILE, lanes]
        y_scr[cb, HALO + TILE:] = jnp.where(seq_last, 0.0, glu[HALO + TILE:, lanes])

    acc = jnp.concatenate([conv_scr[cb] for cb in range(D // LANES)], axis=1)
    mu = jnp.mean(acc, axis=-1, keepdims=True)
    ac = acc - mu
    var = jnp.mean(ac * ac, axis=-1, keepdims=True)
    ln = ac * lax.rsqrt(var + EPS) * lg_ref[...] + lb_ref[...]
    act = (ln * jax.nn.sigmoid(ln)).astype(BF16)
    y = _dot(act, wp2_ref[...]) + bp2_ref[...]

    base = HALO - CONV_K // 2
    rows_in = CONV_ROWS + SUBLANES

    row_blocks = TILE // CONV_ROWS

    def conv_piece(p, anchor):
        cb, r0 = p // row_blocks, (p % row_blocks) * CONV_ROWS
        lanes = slice(cb * LANES, (cb + 1) * LANES)
        w_scr[p, 0:CONV_K, :] = _pin_after(wdw_ref[:, lanes], anchor[0:1, 0:LANES])
        out = bdw_ref[:, lanes]
        for s in range(SUBLANES):
            part = None
            for off in range(s, base + CONV_K, SUBLANES):
                if off < base:
                    continue
                r1 = r0 + off - s
                term = y_scr[cb, r1:r1 + rows_in, :] * w_scr[p, off - base:off - base + 1, :]
                part = term if part is None else part + term
            out = out + part[s:s + CONV_ROWS]
        conv_scr[cb, r0:r0 + CONV_ROWS, :] = out
        return out[0:1, :]

    per_slot = CONV_PIECES // FFN_CHUNKS

    def conv_part(slot, anchor):
        for p in range(slot * per_slot, (slot + 1) * per_slot):
            token = conv_piece(p, anchor)
        return token

    o_ref[...] = _ffn_tail(xb_ref[...], y, modb_ref, ng2_ref, w1_ref, w2_ref, fg_ref,
                           side_work=conv_part, h2_scr=h2_scr)


def _odd_layer(xs, mod3, layer, ng1, ng2, wp1_all, j, bp1, wdw, bdw, lg, lb, wp2_all, bp2, w1_all, w2_all,
               final_g=None):
    n = xs.shape[0]
    n_tiles = n // TILE
    per = TILE // HALO
    halo_blocks = n // HALO
    tile_a = lambda i: jnp.minimum(i, n_tiles - 1)
    tile_b = lambda i: jnp.maximum(i - 1, 0)
    in_specs = [
        pl.BlockSpec((HALO, D), lambda i: (jnp.maximum(tile_a(i) * per - 1, 0), 0)),
        pl.BlockSpec((TILE, D), lambda i: (tile_a(i), 0)),
        pl.BlockSpec((HALO, D), lambda i: (jnp.minimum((tile_a(i) + 1) * per, halo_blocks - 1), 0)),
        pl.BlockSpec((TILE, D), lambda i: (tile_b(i), 0)),
        _mod_row_spec(layer, tile_a), _mod_row_spec(layer, tile_b), _const_spec((1, D)), _const_spec((1, D)),
        IN_HBM, _const_spec((1, 2 * D)), _const_spec((CONV_K, D)), _const_spec((1, D)),
        _const_spec((1, D)), _const_spec((1, D)), IN_HBM, _const_spec((1, D)),
        IN_HBM, IN_HBM,
    ]
    args = [xs, xs, xs, xs, mod3, mod3, ng1, ng2, wp1_all, bp1, wdw, bdw, lg, lb, wp2_all, bp2, w1_all, w2_all]
    if final_g is not None:
        in_specs.append(_const_spec((1, D)))
        args.append(final_g)
    return pl.pallas_call(
        functools.partial(_odd_kernel, final=final_g is not None, n_tiles=n_tiles, layer=layer, j=j),
        grid=(n_tiles + 1,),
        in_specs=in_specs,
        out_specs=pl.BlockSpec((TILE, D), lambda i: (tile_b(i), 0)),
        out_shape=jax.ShapeDtypeStruct((n, D), F32),
        scratch_shapes=[pltpu.VMEM((D // LANES, HALO + TILE + HALO, LANES), F32),
                        pltpu.VMEM((D // LANES, TILE, LANES), F32),
                        pltpu.VMEM((CONV_PIECES, CONV_K + 1, LANES), F32), pltpu.VMEM((TILE, D), BF16),
                        pltpu.VMEM((D, 2 * D), BF16), pltpu.VMEM((D, D), BF16),
                        pltpu.VMEM((D, FFN), BF16), pltpu.VMEM((FFN, D), BF16),
                        pltpu.SemaphoreType.DMA((4,))],
        compiler_params=_params(("arbitrary",)),
    )(*args)


def kernel(x, c, ctx, c_ctx, w_mod, b_mod, norm_g, w_in, w_out, ln_v_g, w_sp, b_sp, rpb, w_pw1, b_pw1,
           w_dw, b_dw, ln_c_g, ln_c_b, w_pw2, b_pw2, w_ff1, w_ff2, final_g):
    row = lambda t: t.reshape(1, -1)
    cc = jnp.concatenate([c, c_ctx[None, :], jnp.zeros((MOD_ROWS - BATCH - 1, D), F32)], axis=0)
    mod3 = _modulation(cc, w_mod, b_mod).reshape(DEPTH * MOD_ROWS, 1, 6 * D)
    xs = (x.reshape(T_LAT, D), ctx.reshape(T_CTX, D))
    w_in, w_out, w_sp, w_pw1, w_pw2, w_ff1, w_ff2 = (
        w.astype(BF16) for w in (w_in, w_out, w_sp, w_pw1, w_pw2, w_ff1, w_ff2))
    last_reader = ((DEPTH - 1) // 2) * 2
    for l in range(DEPTH):
        ctx_out = l < last_reader
        fg = row(final_g) if l == DEPTH - 1 else None
        j = l // 2
        if l % 2 == 0:
            a, q, k, v = _even_in(xs, mod3, l, row(norm_g[l, 0]), w_in, j, row(ln_v_g[j]),
                                  w_sp, b_sp[j][:, :, None])
            b = _attention(q, k, v, _bias_tables(rpb[j]), ctx_out)
            xs = _even_out(a, b, xs, mod3, l, row(norm_g[l, 1]), w_out, j, w_ff1, w_ff2,
                           T_ALL if ctx_out else T_LAT, fg)
        else:
            xs = _odd_layer(xs, mod3, l, row(norm_g[l, 0]), row(norm_g[l, 1]), w_pw1, j,
                            row(b_pw1[j]), w_dw[j], row(b_dw[j]), row(ln_c_g[j]), row(ln_c_b[j]),
                            w_pw2, row(b_pw2[j]), w_ff1, w_ff2, fg)
    return xs.reshape(BATCH, SEQ, D)
```

```python
import functools

import jax
import jax.numpy as jnp
from jax import lax
from jax.experimental import pallas as pl
from jax.experimental.pallas import tpu as pltpu

D = 1024
BATCH = 4
SEQ = 4096
DEPTH = 4
GRID_W = 64
ROWS = SEQ // GRID_W
CTX_LEN = 256
CHUNK = 128
A_GROUPS = 4
A_W = 512
B_W = 512
HEAD_DIM = 64
HEADS = 8
WIN_H = 8
WIN_W = 16
Q_ROWS = 4
Q_BLK = Q_ROWS * GRID_W
K_ROWS = Q_ROWS + WIN_H
K_BLK = K_ROWS * GRID_W
ROW_STEPS = ROWS // Q_ROWS
MIX_IN = 2 * A_W + 3 * B_W
CONV_K = 31
HALO = 16
SUBLANES = 8
LANES = 128
CONV_ROWS = 128
FFN = 4 * D
FFN_CHUNK = 1024
FFN_CHUNKS = FFN // FFN_CHUNK
WBLK = 1024
assert WBLK == D == FFN_CHUNK == A_W + B_W
EPS = 1e-6
NEG = -1e30

T_LAT = BATCH * SEQ
T_CTX = BATCH * CTX_LEN
T_ALL = T_LAT + T_CTX
TILE = 256
IN_TILE = 1024
OUT_TILE = 512
CONV_PIECES = (D // LANES) * (TILE // CONV_ROWS)
TILES_PER_SEQ = SEQ // TILE
LAT_TILES = T_LAT // TILE
MOD_ROWS = 8
MOD_N = 1536
VMEM_LIMIT = 56 * 1024 * 1024

BF16 = jnp.bfloat16
F32 = jnp.float32


def _const_spec(shape):
    zeros = (0,) * len(shape)
    return pl.BlockSpec(shape, lambda *_: zeros, pipeline_mode=pl.Buffered(1))


def _layer_spec(stacked, idx):
    shape = stacked.shape[1:]
    zeros = (0,) * len(shape)
    return pl.BlockSpec((None,) + shape, lambda *_: (idx,) + zeros, pipeline_mode=pl.Buffered(1))


def _params(sem):
    return pltpu.CompilerParams(dimension_semantics=sem, vmem_limit_bytes=VMEM_LIMIT)


def _mod_row_spec(layer, tile_of=lambda i: i, tiles_per_seq=TILES_PER_SEQ):
    return pl.BlockSpec(
        (None, 1, 6 * D),
        lambda i: (layer * MOD_ROWS + jnp.minimum(tile_of(i) // tiles_per_seq, BATCH), 0, 0))


def _rms_mod(x, g, shift, scale):
    return x * lax.rsqrt(jnp.mean(x * x, axis=-1, keepdims=True) + EPS) * (g * (1.0 + scale)) + shift


def _dot(a, b):
    return jnp.dot(a, b, preferred_element_type=F32)


def _dot_nt(a, b):
    return lax.dot_general(a, b, (((1,), (1,)), ((), ())), preferred_element_type=F32)


def _mod_kernel(cc_ref, w_ref, b_ref, o_ref):
    cc = cc_ref[...]
    sc = (cc * jax.nn.sigmoid(cc)).astype(BF16)
    o_ref[...] = _dot(sc, w_ref[...].astype(BF16)) + b_ref[...]


def _modulation(cc, w_mod, b_mod):
    nblk = (6 * D) // MOD_N
    return pl.pallas_call(
        _mod_kernel,
        grid=(DEPTH, nblk),
        in_specs=[
            pl.BlockSpec((MOD_ROWS, D), lambda l, n: (0, 0)),
            pl.BlockSpec((None, D, MOD_N), lambda l, n: (l, 0, n)),
            pl.BlockSpec((None, 1, MOD_N), lambda l, n: (l, 0, n)),
        ],
        out_specs=pl.BlockSpec((None, MOD_ROWS, MOD_N), lambda l, n: (l, 0, n)),
        out_shape=jax.ShapeDtypeStruct((DEPTH, MOD_ROWS, 6 * D), F32),
        compiler_params=_params(("arbitrary", "arbitrary")),
    )(cc, w_mod, b_mod.reshape(DEPTH, 1, 6 * D))


def _load_stream(x_refs, lat_tiles):
    if len(x_refs) == 1:
        return x_refs[0][...]
    return jnp.where(pl.program_id(0) < lat_tiles, x_refs[0][...], x_refs[1][...])


def _stream_specs(xs, tile):
    if not isinstance(xs, tuple):
        return [pl.BlockSpec((tile, D), lambda i: (i, 0))], [xs]
    lat_tiles = xs[0].shape[0] // tile
    return [pl.BlockSpec((tile, D), lambda i: (jnp.minimum(i, lat_tiles - 1), 0)),
            pl.BlockSpec((tile, D), lambda i: (jnp.maximum(i - lat_tiles, 0), 0))], list(xs)


def _even_in_kernel(*refs, n_x):
    x_refs = refs[:n_x]
    mod_ref, ng_ref, w_in_ref, lng_ref, w_sp_ref, b_sp_ref, a_ref, q_ref, k_ref, v_ref = refs[n_x:]
    x = _load_stream(x_refs, T_LAT // IN_TILE)
    h = _rms_mod(x, ng_ref[...], mod_ref[:, 0:D], mod_ref[:, D:2 * D]).astype(BF16)
    z = jax.nn.gelu(_dot(h, w_in_ref[:, 0:2 * A_W]))
    gd = A_W // A_GROUPS
    for grp in range(A_GROUPS):
        g = z[:, A_W + grp * gd:A_W + (grp + 1) * gd]
        mu = jnp.mean(g, axis=-1, keepdims=True)
        gc = g - mu
        var = jnp.mean(gc * gc, axis=-1, keepdims=True)
        gn = (gc * lax.rsqrt(var + EPS) * lng_ref[:, grp * gd:(grp + 1) * gd]).astype(BF16)
        for ch in range(IN_TILE // CHUNK):
            rows = slice(ch * CHUNK, (ch + 1) * CHUNK)
            s = _dot(w_sp_ref[grp], gn[rows]) + b_sp_ref[grp]
            a_ref[rows, grp * gd:(grp + 1) * gd] = (z[rows, grp * gd:(grp + 1) * gd] * s).astype(BF16)
    c0 = 2 * A_W
    q_ref[...] = (_dot(h, w_in_ref[:, c0:c0 + B_W]) * (HEAD_DIM ** -0.5)).astype(BF16)
    k_ref[...] = _dot(h, w_in_ref[:, c0 + B_W:c0 + 2 * B_W]).astype(BF16)
    v_ref[...] = _dot(h, w_in_ref[:, c0 + 2 * B_W:c0 + 3 * B_W]).astype(BF16)


def _even_in(xs, mod3, layer, ng, w_in_all, j, lng, w_sp_all, b_sp):
    x_specs, x_args = _stream_specs(xs, IN_TILE)
    tok = lambda w: pl.BlockSpec((IN_TILE, w), lambda i: (i, 0))
    out = jax.ShapeDtypeStruct((T_ALL, A_W), BF16)
    return pl.pallas_call(
        functools.partial(_even_in_kernel, n_x=len(x_args)),
        grid=(T_ALL // IN_TILE,),
        in_specs=x_specs + [
            _mod_row_spec(layer, tiles_per_seq=SEQ // IN_TILE), _const_spec((1, D)), _layer_spec(w_in_all, j),
            _const_spec((1, A_W)), _layer_spec(w_sp_all, j), _const_spec((A_GROUPS, CHUNK, 1)),
        ],
        out_specs=[tok(A_W)] * 4,
        out_shape=[out] * 4,
        compiler_params=_params(("arbitrary",)),
    )(*x_args, mod3, ng, w_in_all, lng, w_sp_all, b_sp)


def _attn_kernel(q_ref, k_ref, v_ref, kc_ref, vc_ref, tbl_ref, o_ref, *, ctx_queries):
    r = pl.program_id(1)
    lane = lax.broadcasted_iota(jnp.int32, (Q_BLK, 2 * HEAD_DIM), 1)
    first_head = lane < HEAD_DIM

    def attend(local):
        if local:
            start = pl.multiple_of(
                jnp.clip(r * Q_ROWS - WIN_H // 2, 0, ROWS - K_ROWS) * GRID_W, GRID_W)
        for hp in range(HEADS // 2):
            cols = slice(hp * 2 * HEAD_DIM, (hp + 1) * 2 * HEAD_DIM)
            q2 = q_ref[:, cols]
            k2 = kc_ref[:, cols]
            v2 = vc_ref[:, cols]
            if local:
                k2 = jnp.concatenate([k_ref[pl.ds(start, K_BLK), cols], k2], axis=0)
                v2 = jnp.concatenate([v_ref[pl.ds(start, K_BLK), cols], v2], axis=0)
            outs = []
            for half in range(2):
                qm = jnp.where(first_head if half == 0 else ~first_head, q2, jnp.zeros_like(q2))
                s = _dot_nt(qm, k2)
                if local:
                    s = jnp.concatenate([s[:, 0:K_BLK] + tbl_ref[2 * hp + half], s[:, K_BLK:]], axis=1)
                m = jnp.max(s, axis=-1, keepdims=True)
                p = jnp.exp(s - m)
                den = jnp.sum(p, axis=-1, keepdims=True)
                outs.append(_dot(p.astype(BF16), v2) / den)
            o_ref[:, cols] = jnp.where(first_head, outs[0], outs[1]).astype(BF16)

    if ctx_queries:
        pl.when(r < ROW_STEPS)(lambda: attend(True))
        pl.when(r >= ROW_STEPS)(lambda: attend(False))
    else:
        attend(True)


def _attention(q, k, v, tbl, ctx_queries):
    steps = ROW_STEPS + (1 if ctx_queries else 0)
    lat_blocks = T_LAT // Q_BLK

    def q_idx(b, r):
        return (jnp.where(r < ROW_STEPS, b * ROW_STEPS + r, lat_blocks + b), 0)

    def tbl_idx(b, r):
        return (jnp.where(r == 0, 0, jnp.where(r >= ROW_STEPS - 1, 2, 1)), 0, 0, 0)

    ctx_blk = T_LAT // CTX_LEN
    return pl.pallas_call(
        functools.partial(_attn_kernel, ctx_queries=ctx_queries),
        grid=(BATCH, steps),
        in_specs=[
            pl.BlockSpec((Q_BLK, B_W), q_idx),
            pl.BlockSpec((SEQ, B_W), lambda b, r: (b, 0)),
            pl.BlockSpec((SEQ, B_W), lambda b, r: (b, 0)),
            pl.BlockSpec((CTX_LEN, B_W), lambda b, r: (ctx_blk + b, 0)),
            pl.BlockSpec((CTX_LEN, B_W), lambda b, r: (ctx_blk + b, 0)),
            pl.BlockSpec((None, HEADS, Q_BLK, K_BLK), tbl_idx),
        ],
        out_specs=pl.BlockSpec((Q_BLK, B_W), q_idx),
        out_shape=jax.ShapeDtypeStruct((T_ALL if ctx_queries else T_LAT, B_W), BF16),
        compiler_params=_params(("arbitrary", "arbitrary")),
    )(q, k, v, k, v, tbl)


N_DR = 2 * WIN_H - 1
N_DC = 2 * WIN_W - 1
TBL_R0 = (0, Q_ROWS, ROWS - Q_ROWS)


def _tbl_kernel(rpb_ref, o_ref, t_scr):
    h = pl.program_id(0)
    shape = (GRID_W, 2 * GRID_W)
    cq = lax.broadcasted_iota(jnp.int32, shape, 0)
    lane = lax.broadcasted_iota(jnp.int32, shape, 1)
    ck = lane & (GRID_W - 1)
    dc = jnp.clip(ck - cq + WIN_W - 1, 0, N_DC - 1)
    cs = jnp.clip(cq - WIN_W // 2, 0, GRID_W - WIN_W)
    col_ok = (ck >= cs) & (ck < cs + WIN_W)
    neg = jnp.full(shape, NEG, F32)
    for dr in range(N_DR):
        t = neg
        for j in range(N_DC):
            t = jnp.where(dc == j, rpb_ref[(h * N_DR + dr) * N_DC + j], t)
        t_scr[dr] = jnp.where(col_ok, t, NEG)
    left = lane < GRID_W
    for v, r0 in enumerate(TBL_R0):
        k0 = min(max(r0 - WIN_H // 2, 0), ROWS - K_ROWS)
        for qi in range(Q_ROWS):
            qrow = r0 + qi
            w0 = min(max(qrow - WIN_H // 2, 0), ROWS - WIN_H)
            for kp in range(K_ROWS // 2):
                halves = []
                for krow in (k0 + 2 * kp, k0 + 2 * kp + 1):
                    in_win = w0 <= krow < w0 + WIN_H
                    halves.append(t_scr[krow - qrow + WIN_H - 1] if in_win else neg)
                o_ref[v, qi * GRID_W:(qi + 1) * GRID_W, kp * 2 * GRID_W:(kp + 1) * 2 * GRID_W] = (
                    jnp.where(left, halves[0], halves[1]))


def _bias_tables(rpb):
    return pl.pallas_call(
        _tbl_kernel,
        grid=(HEADS,),
        in_specs=[pl.BlockSpec(memory_space=pltpu.SMEM)],
        out_specs=pl.BlockSpec((len(TBL_R0), None, Q_BLK, K_BLK), lambda h: (0, h, 0, 0)),
        out_shape=jax.ShapeDtypeStruct((len(TBL_R0), HEADS, Q_BLK, K_BLK), F32),
        scratch_shapes=[pltpu.VMEM((N_DR, GRID_W, 2 * GRID_W), F32)],
        compiler_params=_params(("arbitrary",)),
    )(rpb.reshape(-1))


def _pin_after(value, anchor):
    bits = lambda t: pltpu.bitcast(t, jnp.uint32)
    zero = lax.shift_right_logical(lax.shift_right_logical(bits(anchor), jnp.uint32(16)), jnp.uint32(16))
    return pltpu.bitcast(bits(value) | zero, F32)


def _ffn_tail(x, y, mod_ref, ng2_ref, w_all, first_blk, fg_ref, side_work=None, h2_scr=None):
    x1 = x + mod_ref[:, 2 * D:3 * D] * y
    h2f = _rms_mod(x1, ng2_ref[...], mod_ref[:, 3 * D:4 * D], mod_ref[:, 4 * D:5 * D])
    h2 = h2f.astype(BF16)
    if side_work is not None:
        h2_scr[...] = h2
    acc = None
    token = None
    for c in range(FFN_CHUNKS):
        if side_work is not None:
            if token is not None:
                h2_scr[0:2 * SUBLANES, 0:LANES] = _pin_after(h2f[0:2 * SUBLANES, 0:LANES], token).astype(BF16)
                h2 = h2_scr[...]
            token = side_work(c, y if c == 0 else acc)
        hid = jnp.maximum(_dot(h2, w_all[first_blk + c]), 0.0)
        part = _dot((hid * hid).astype(BF16), w_all[first_blk + FFN_CHUNKS + c])
        acc = part if acc is None else acc + part
    x2 = x1 + mod_ref[:, 5 * D:6 * D] * acc
    if fg_ref is not None:
        x2 = x2 * lax.rsqrt(jnp.mean(x2 * x2, axis=-1, keepdims=True) + EPS) * fg_ref[...]
    return x2


IN_HBM = pl.BlockSpec(memory_space=pl.ANY)


def _ffn_blocks(w1_hbm, w2_hbm, layer):
    return ([w1_hbm.at[layer, :, pl.ds(c * WBLK, WBLK)] for c in range(FFN_CHUNKS)]
            + [w2_hbm.at[layer, pl.ds(c * WBLK, WBLK), :] for c in range(FFN_CHUNKS)])


def _stage_weights(srcs, w_all, stage, sem):
    copies = [pltpu.make_async_copy(src, stage.at[n % 2], sem.at[n % 2]) for n, src in enumerate(srcs)]
    copies[0].start()
    for n in range(len(srcs)):
        if n + 1 < len(srcs):
            copies[n + 1].start()
        copies[n].wait()
        w_all[n] = stage[n % 2].astype(BF16)


def _even_out_kernel(*refs, final, n_x, layer, j):
    x_refs, refs = refs[:n_x], refs[n_x:]
    a_ref, b_ref, mod_ref, ng2_ref, wo_hbm, w1_hbm, w2_hbm = refs[:7]
    fg_ref = refs[7] if final else None
    o_ref, w_all, stage, w_sem = refs[-4:]

    @pl.when(pl.program_id(0) == 0)
    def _():
        _stage_weights([wo_hbm.at[j]] + _ffn_blocks(w1_hbm, w2_hbm, layer), w_all, stage, w_sem)

    y = _dot(a_ref[...], w_all[0, 0:A_W, :]) + _dot(b_ref[...], w_all[0, A_W:A_W + B_W, :])
    o_ref[...] = _ffn_tail(_load_stream(x_refs, T_LAT // OUT_TILE), y, mod_ref, ng2_ref, w_all, 1, fg_ref)


def _even_out(a, b, xs, mod3, layer, ng2, w_out_all, j, w1_all, w2_all, n_out, final_g=None):
    x_specs, x_args = _stream_specs(xs, OUT_TILE)
    tok = lambda w: pl.BlockSpec((OUT_TILE, w), lambda i: (i, 0))
    in_specs = x_specs + [tok(A_W), tok(B_W), _mod_row_spec(layer, tiles_per_seq=SEQ // OUT_TILE),
                          _const_spec((1, D)), IN_HBM, IN_HBM, IN_HBM]
    args = x_args + [a, b, mod3, ng2, w_out_all, w1_all, w2_all]
    if final_g is not None:
        in_specs.append(_const_spec((1, D)))
        args.append(final_g)
    return pl.pallas_call(
        functools.partial(_even_out_kernel, final=final_g is not None, n_x=len(x_args), layer=layer, j=j),
        grid=(n_out // OUT_TILE,),
        in_specs=in_specs,
        out_specs=tok(D),
        out_shape=jax.ShapeDtypeStruct((n_out, D), F32),
        scratch_shapes=[pltpu.VMEM((1 + 2 * FFN_CHUNKS, WBLK, WBLK), BF16), pltpu.VMEM((2, WBLK, WBLK), F32),
                        pltpu.SemaphoreType.DMA((2,))],
        compiler_params=_params(("arbitrary",)),
    )(*args)


def _odd_kernel(*refs, final, n_tiles, layer, j):
    (xp_ref, xa_ref, xn_ref, xb_ref, moda_ref, modb_ref, ng1_ref, ng2_ref, wp1_hbm, bp1_ref, wdw_ref,
     bdw_ref, lg_ref, lb_ref, wp2_hbm, bp2_ref, w1_hbm, w2_hbm) = refs[:18]
    fg_ref = refs[18] if final else None
    o_ref, y_scr, conv_scr, w_scr, h2_scr, w_all, stage, w_sem = refs[-8:]
    i = pl.program_id(0)

    @pl.when(i == 0)
    def _():
        conv_scr[...] = jnp.zeros_like(conv_scr)
        _stage_weights([wp1_hbm.at[j, :, pl.ds(0, D)], wp1_hbm.at[j, :, pl.ds(D, D)], wp2_hbm.at[j]]
                       + _ffn_blocks(w1_hbm, w2_hbm, layer), w_all, stage, w_sem)

    ia = jnp.minimum(i, n_tiles - 1)
    is_lat = ia < LAT_TILES
    pos = ia % TILES_PER_SEQ
    seq_first = jnp.logical_or(jnp.logical_not(is_lat), pos == 0)
    seq_last = jnp.logical_or(jnp.logical_not(is_lat), pos == TILES_PER_SEQ - 1)
    xe = jnp.concatenate([xp_ref[...], xa_ref[...], xn_ref[...]], axis=0)
    h = _rms_mod(xe, ng1_ref[...], moda_ref[:, 0:D], moda_ref[:, D:2 * D]).astype(BF16)
    glu = (_dot(h, w_all[0]) + bp1_ref[:, 0:D]) * jax.nn.sigmoid(_dot(h, w_all[1]) + bp1_ref[:, D:2 * D])
    for cb in range(D // LANES):
        lanes = slice(cb * LANES, (cb + 1) * LANES)
        y_scr[cb, 0:HALO] = jnp.where(seq_first, 0.0, glu[0:HALO, lanes])
        y_scr[cb, HALO:HALO + TILE] = glu[HALO:HALO + TILE, lanes]
        y_scr[cb, HALO + TILE:] = jnp.where(seq_last, 0.0, glu[HALO + TILE:, lanes])

    acc = jnp.concatenate([conv_scr[cb] for cb in range(D // LANES)], axis=1)
    mu = jnp.mean(acc, axis=-1, keepdims=True)
    ac = acc - mu
    var = jnp.mean(ac * ac, axis=-1, keepdims=True)
    ln = ac * lax.rsqrt(var + EPS) * lg_ref[...] + lb_ref[...]
    act = (ln * jax.nn.sigmoid(ln)).astype(BF16)
    y = _dot(act, w_all[2]) + bp2_ref[...]

    base = HALO - CONV_K // 2
    rows_in = CONV_ROWS + SUBLANES

    row_blocks = TILE // CONV_ROWS

    def conv_piece(p, anchor):
        cb, r0 = p // row_blocks, (p % row_blocks) * CONV_ROWS
        lanes = slice(cb * LANES, (cb + 1) * LANES)
        w_scr[p, 0:CONV_K, :] = _pin_after(wdw_ref[:, lanes], anchor[0:1, 0:LANES])
        out = bdw_ref[:, lanes]
        for s in range(SUBLANES):
            part = None
            for off in range(s, base + CONV_K, SUBLANES):
                if off < base:
                    continue
                r1 = r0 + off - s
                term = y_scr[cb, r1:r1 + rows_in, :] * w_scr[p, off - base:off - base + 1, :]
                part = term if part is None else part + term
            out = out + part[s:s + CONV_ROWS]
        conv_scr[cb, r0:r0 + CONV_ROWS, :] = out
        return out[0:1, :]

    per_slot = CONV_PIECES // FFN_CHUNKS

    def conv_part(slot, anchor):
        for p in range(slot * per_slot, (slot + 1) * per_slot):
            token = conv_piece(p, anchor)
        return token

    o_ref[...] = _ffn_tail(xb_ref[...], y, modb_ref, ng2_ref, w_all, 3, fg_ref,
                           side_work=conv_part, h2_scr=h2_scr)


def _odd_layer(xs, mod3, layer, ng1, ng2, wp1_all, j, bp1, wdw, bdw, lg, lb, wp2_all, bp2, w1_all, w2_all,
               final_g=None):
    n = xs.shape[0]
    n_tiles = n // TILE
    per = TILE // HALO
    halo_blocks = n // HALO
    tile_a = lambda i: jnp.minimum(i, n_tiles - 1)
    tile_b = lambda i: jnp.maximum(i - 1, 0)
    in_specs = [
        pl.BlockSpec((HALO, D), lambda i: (jnp.maximum(tile_a(i) * per - 1, 0), 0)),
        pl.BlockSpec((TILE, D), lambda i: (tile_a(i), 0)),
        pl.BlockSpec((HALO, D), lambda i: (jnp.minimum((tile_a(i) + 1) * per, halo_blocks - 1), 0)),
        pl.BlockSpec((TILE, D), lambda i: (tile_b(i), 0)),
        _mod_row_spec(layer, tile_a), _mod_row_spec(layer, tile_b), _const_spec((1, D)), _const_spec((1, D)),
        IN_HBM, _const_spec((1, 2 * D)), _const_spec((CONV_K, D)), _const_spec((1, D)),
        _const_spec((1, D)), _const_spec((1, D)), IN_HBM, _const_spec((1, D)),
        IN_HBM, IN_HBM,
    ]
    args = [xs, xs, xs, xs, mod3, mod3, ng1, ng2, wp1_all, bp1, wdw, bdw, lg, lb, wp2_all, bp2, w1_all, w2_all]
    if final_g is not None:
        in_specs.append(_const_spec((1, D)))
        args.append(final_g)
    return pl.pallas_call(
        functools.partial(_odd_kernel, final=final_g is not None, n_tiles=n_tiles, layer=layer, j=j),
        grid=(n_tiles + 1,),
        in_specs=in_specs,
        out_specs=pl.BlockSpec((TILE, D), lambda i: (tile_b(i), 0)),
        out_shape=jax.ShapeDtypeStruct((n, D), F32),
        scratch_shapes=[pltpu.VMEM((D // LANES, HALO + TILE + HALO, LANES), F32),
                        pltpu.VMEM((D // LANES, TILE, LANES), F32),
                        pltpu.VMEM((CONV_PIECES, CONV_K + 1, LANES), F32), pltpu.VMEM((TILE, D), BF16),
                        pltpu.VMEM((3 + 2 * FFN_CHUNKS, WBLK, WBLK), BF16), pltpu.VMEM((2, WBLK, WBLK), F32),
                        pltpu.SemaphoreType.DMA((2,))],
        compiler_params=_params(("arbitrary",)),
    )(*args)


def kernel(x, c, ctx, c_ctx, w_mod, b_mod, norm_g, w_in, w_out, ln_v_g, w_sp, b_sp, rpb, w_pw1, b_pw1,
           w_dw, b_dw, ln_c_g, ln_c_b, w_pw2, b_pw2, w_ff1, w_ff2, final_g):
    row = lambda t: t.reshape(1, -1)
    cc = jnp.concatenate([c, c_ctx[None, :], jnp.zeros((MOD_ROWS - BATCH - 1, D), F32)], axis=0)
    mod3 = _modulation(cc, w_mod, b_mod).reshape(DEPTH * MOD_ROWS, 1, 6 * D)
    xs = (x.reshape(T_LAT, D), ctx.reshape(T_CTX, D))
    w_in, w_sp = w_in.astype(BF16), w_sp.astype(BF16)
    last_reader = ((DEPTH - 1) // 2) * 2
    for l in range(DEPTH):
        ctx_out = l < last_reader
        fg = row(final_g) if l == DEPTH - 1 else None
        j = l // 2
        if l % 2 == 0:
            a, q, k, v = _even_in(xs, mod3, l, row(norm_g[l, 0]), w_in, j, row(ln_v_g[j]),
                                  w_sp, b_sp[j][:, :, None])
            b = _attention(q, k, v, _bias_tables(rpb[j]), ctx_out)
            xs = _even_out(a, b, xs, mod3, l, row(norm_g[l, 1]), w_out, j, w_ff1, w_ff2,
                           T_ALL if ctx_out else T_LAT, fg)
        else:
            xs = _odd_layer(xs, mod3, l, row(norm_g[l, 0]), row(norm_g[l, 1]), w_pw1, j,
                            row(b_pw1[j]), w_dw[j], row(b_dw[j]), row(ln_c_g[j]), row(ln_c_b[j]),
                            w_pw2, row(b_pw2[j]), w_ff1, w_ff2, fg)
    return xs.reshape(BATCH, SEQ, D)
```

```python
import functools

import jax
import jax.numpy as jnp
from jax import lax
from jax.experimental import pallas as pl
from jax.experimental.pallas import tpu as pltpu

D = 1024
BATCH = 4
SEQ = 4096
DEPTH = 4
GRID_W = 64
ROWS = SEQ // GRID_W
CTX_LEN = 256
CHUNK = 128
A_GROUPS = 4
A_W = 512
B_W = 512
HEAD_DIM = 64
HEADS = 8
WIN_H = 8
WIN_W = 16
Q_ROWS = 4
Q_BLK = Q_ROWS * GRID_W
K_ROWS = Q_ROWS + WIN_H
K_BLK = K_ROWS * GRID_W
ROW_STEPS = ROWS // Q_ROWS
MIX_IN = 2 * A_W + 3 * B_W
CONV_K = 31
HALO = 16
SUBLANES = 8
LANES = 128
CONV_ROWS = 128
FFN = 4 * D
FFN_CHUNK = 1024
FFN_CHUNKS = FFN // FFN_CHUNK
WBLK = 1024
assert WBLK == D == FFN_CHUNK == A_W + B_W
EPS = 1e-6
NEG = -1e30

T_LAT = BATCH * SEQ
T_CTX = BATCH * CTX_LEN
T_ALL = T_LAT + T_CTX
TILE = 256
IN_TILE = 1024
OUT_TILE = 512
CONV_PIECES = (D // LANES) * (TILE // CONV_ROWS)
TILES_PER_SEQ = SEQ // TILE
LAT_TILES = T_LAT // TILE
MOD_ROWS = 8
MOD_N = 1536
VMEM_LIMIT = 56 * 1024 * 1024

BF16 = jnp.bfloat16
F32 = jnp.float32


def _const_spec(shape):
    zeros = (0,) * len(shape)
    return pl.BlockSpec(shape, lambda *_: zeros, pipeline_mode=pl.Buffered(1))


def _layer_spec(stacked, idx):
    shape = stacked.shape[1:]
    zeros = (0,) * len(shape)
    return pl.BlockSpec((None,) + shape, lambda *_: (idx,) + zeros, pipeline_mode=pl.Buffered(1))


def _params(sem):
    return pltpu.CompilerParams(dimension_semantics=sem, vmem_limit_bytes=VMEM_LIMIT)


def _mod_row_spec(layer, tile_of=lambda i: i, tiles_per_seq=TILES_PER_SEQ):
    return pl.BlockSpec(
        (None, 1, 6 * D),
        lambda i: (layer * MOD_ROWS + jnp.minimum(tile_of(i) // tiles_per_seq, BATCH), 0, 0))


def _rms_mod(x, g, shift, scale):
    return x * lax.rsqrt(jnp.mean(x * x, axis=-1, keepdims=True) + EPS) * (g * (1.0 + scale)) + shift


def _dot(a, b):
    return jnp.dot(a, b, preferred_element_type=F32)


def _dot_nt(a, b):
    return lax.dot_general(a, b, (((1,), (1,)), ((), ())), preferred_element_type=F32)


def _mod_kernel(cc_ref, w_ref, b_ref, o_ref):
    cc = cc_ref[...]
    sc = (cc * jax.nn.sigmoid(cc)).astype(BF16)
    o_ref[...] = _dot(sc, w_ref[...].astype(BF16)) + b_ref[...]


def _modulation(cc, w_mod, b_mod):
    nblk = (6 * D) // MOD_N
    return pl.pallas_call(
        _mod_kernel,
        grid=(DEPTH, nblk),
        in_specs=[
            pl.BlockSpec((MOD_ROWS, D), lambda l, n: (0, 0)),
            pl.BlockSpec((None, D, MOD_N), lambda l, n: (l, 0, n)),
            pl.BlockSpec((None, 1, MOD_N), lambda l, n: (l, 0, n)),
        ],
        out_specs=pl.BlockSpec((None, MOD_ROWS, MOD_N), lambda l, n: (l, 0, n)),
        out_shape=jax.ShapeDtypeStruct((DEPTH, MOD_ROWS, 6 * D), F32),
        compiler_params=_params(("arbitrary", "arbitrary")),
    )(cc, w_mod, b_mod.reshape(DEPTH, 1, 6 * D))


def _load_stream(x_refs, lat_tiles):
    if len(x_refs) == 1:
        return x_refs[0][...]
    return jnp.where(pl.program_id(0) < lat_tiles, x_refs[0][...], x_refs[1][...])


def _stream_specs(xs, tile):
    if not isinstance(xs, tuple):
        return [pl.BlockSpec((tile, D), lambda i: (i, 0))], [xs]
    lat_tiles = xs[0].shape[0] // tile
    return [pl.BlockSpec((tile, D), lambda i: (jnp.minimum(i, lat_tiles - 1), 0)),
            pl.BlockSpec((tile, D), lambda i: (jnp.maximum(i - lat_tiles, 0), 0))], list(xs)


def _even_in_kernel(*refs, n_x):
    x_refs = refs[:n_x]
    mod_ref, ng_ref, w_in_ref, lng_ref, w_sp_ref, b_sp_ref, a_ref, q_ref, k_ref, v_ref = refs[n_x:]
    x = _load_stream(x_refs, T_LAT // IN_TILE)
    h = _rms_mod(x, ng_ref[...], mod_ref[:, 0:D], mod_ref[:, D:2 * D]).astype(BF16)
    z = jax.nn.gelu(_dot(h, w_in_ref[:, 0:2 * A_W]))
    gd = A_W // A_GROUPS
    for grp in range(A_GROUPS):
        g = z[:, A_W + grp * gd:A_W + (grp + 1) * gd]
        mu = jnp.mean(g, axis=-1, keepdims=True)
        gc = g - mu
        var = jnp.mean(gc * gc, axis=-1, keepdims=True)
        gn = (gc * lax.rsqrt(var + EPS) * lng_ref[:, grp * gd:(grp + 1) * gd]).astype(BF16)
        for ch in range(IN_TILE // CHUNK):
            rows = slice(ch * CHUNK, (ch + 1) * CHUNK)
            s = _dot(w_sp_ref[grp], gn[rows]) + b_sp_ref[grp]
            a_ref[rows, grp * gd:(grp + 1) * gd] = (z[rows, grp * gd:(grp + 1) * gd] * s).astype(BF16)
    c0 = 2 * A_W
    q_ref[...] = (_dot(h, w_in_ref[:, c0:c0 + B_W]) * (HEAD_DIM ** -0.5)).astype(BF16)
    k_ref[...] = _dot(h, w_in_ref[:, c0 + B_W:c0 + 2 * B_W]).astype(BF16)
    v_ref[...] = _dot(h, w_in_ref[:, c0 + 2 * B_W:c0 + 3 * B_W]).astype(BF16)


def _even_in(xs, mod3, layer, ng, w_in_all, j, lng, w_sp_all, b_sp):
    x_specs, x_args = _stream_specs(xs, IN_TILE)
    tok = lambda w: pl.BlockSpec((IN_TILE, w), lambda i: (i, 0))
    out = jax.ShapeDtypeStruct((T_ALL, A_W), BF16)
    return pl.pallas_call(
        functools.partial(_even_in_kernel, n_x=len(x_args)),
        grid=(T_ALL // IN_TILE,),
        in_specs=x_specs + [
            _mod_row_spec(layer, tiles_per_seq=SEQ // IN_TILE), _const_spec((1, D)), _layer_spec(w_in_all, j),
            _const_spec((1, A_W)), _layer_spec(w_sp_all, j), _const_spec((A_GROUPS, CHUNK, 1)),
        ],
        out_specs=[tok(A_W)] * 4,
        out_shape=[out] * 4,
        compiler_params=_params(("arbitrary",)),
    )(*x_args, mod3, ng, w_in_all, lng, w_sp_all, b_sp)


def _attn_kernel(q_ref, k_ref, v_ref, kc_ref, vc_ref, tbl_ref, o_ref, *, ctx_queries):
    r = pl.program_id(1)
    lane = lax.broadcasted_iota(jnp.int32, (Q_BLK, 2 * HEAD_DIM), 1)
    first_head = lane < HEAD_DIM

    def attend(local):
        if local:
            start = pl.multiple_of(
                jnp.clip(r * Q_ROWS - WIN_H // 2, 0, ROWS - K_ROWS) * GRID_W, GRID_W)
        for hp in range(HEADS // 2):
            cols = slice(hp * 2 * HEAD_DIM, (hp + 1) * 2 * HEAD_DIM)
            q2 = q_ref[:, cols]
            k2 = kc_ref[:, cols]
            v2 = vc_ref[:, cols]
            if local:
                k2 = jnp.concatenate([k_ref[pl.ds(start, K_BLK), cols], k2], axis=0)
                v2 = jnp.concatenate([v_ref[pl.ds(start, K_BLK), cols], v2], axis=0)
            zero = jnp.zeros_like(q2)
            s2 = _dot_nt(jnp.concatenate([jnp.where(first_head, q2, zero), jnp.where(first_head, zero, q2)],
                                         axis=0), k2)
            ps, dens = [], []
            for half in range(2):
                s = s2[half * Q_BLK:(half + 1) * Q_BLK]
                if local:
                    s = jnp.concatenate([s[:, 0:K_BLK] + tbl_ref[2 * hp + half], s[:, K_BLK:]], axis=1)
                m = jnp.max(s, axis=-1, keepdims=True)
                p = jnp.exp(s - m)
                dens.append(jnp.sum(p, axis=-1, keepdims=True))
                ps.append(p.astype(BF16))
            o2 = _dot(jnp.concatenate(ps, axis=0), v2)
            o_ref[:, cols] = jnp.where(first_head, o2[0:Q_BLK] / dens[0], o2[Q_BLK:] / dens[1]).astype(BF16)

    if ctx_queries:
        pl.when(r < ROW_STEPS)(lambda: attend(True))
        pl.when(r >= ROW_STEPS)(lambda: attend(False))
    else:
        attend(True)


def _attention(q, k, v, tbl, ctx_queries):
    steps = ROW_STEPS + (1 if ctx_queries else 0)
    lat_blocks = T_LAT // Q_BLK

    def q_idx(b, r):
        return (jnp.where(r < ROW_STEPS, b * ROW_STEPS + r, lat_blocks + b), 0)

    def tbl_idx(b, r):
        return (jnp.where(r == 0, 0, jnp.where(r >= ROW_STEPS - 1, 2, 1)), 0, 0, 0)

    ctx_blk = T_LAT // CTX_LEN
    return pl.pallas_call(
        functools.partial(_attn_kernel, ctx_queries=ctx_queries),
        grid=(BATCH, steps),
        in_specs=[
            pl.BlockSpec((Q_BLK, B_W), q_idx),
            pl.BlockSpec((SEQ, B_W), lambda b, r: (b, 0)),
            pl.BlockSpec((SEQ, B_W), lambda b, r: (b, 0)),
            pl.BlockSpec((CTX_LEN, B_W), lambda b, r: (ctx_blk + b, 0)),
            pl.BlockSpec((CTX_LEN, B_W), lambda b, r: (ctx_blk + b, 0)),
            pl.BlockSpec((None, HEADS, Q_BLK, K_BLK), tbl_idx),
        ],
        out_specs=pl.BlockSpec((Q_BLK, B_W), q_idx),
        out_shape=jax.ShapeDtypeStruct((T_ALL if ctx_queries else T_LAT, B_W), BF16),
        compiler_params=_params(("arbitrary", "arbitrary")),
    )(q, k, v, k, v, tbl)


N_DR = 2 * WIN_H - 1
N_DC = 2 * WIN_W - 1
TBL_R0 = (0, Q_ROWS, ROWS - Q_ROWS)


def _tbl_kernel(rpb_ref, o_ref, t_scr):
    h = pl.program_id(0)
    shape = (GRID_W, 2 * GRID_W)
    cq = lax.broadcasted_iota(jnp.int32, shape, 0)
    lane = lax.broadcasted_iota(jnp.int32, shape, 1)
    ck = lane & (GRID_W - 1)
    dc = jnp.clip(ck - cq + WIN_W - 1, 0, N_DC - 1)
    cs = jnp.clip(cq - WIN_W // 2, 0, GRID_W - WIN_W)
    col_ok = (ck >= cs) & (ck < cs + WIN_W)
    neg = jnp.full(shape, NEG, F32)
    for dr in range(N_DR):
        t = neg
        for j in range(N_DC):
            t = jnp.where(dc == j, rpb_ref[(h * N_DR + dr) * N_DC + j], t)
        t_scr[dr] = jnp.where(col_ok, t, NEG)
    left = lane < GRID_W
    for v, r0 in enumerate(TBL_R0):
        k0 = min(max(r0 - WIN_H // 2, 0), ROWS - K_ROWS)
        for qi in range(Q_ROWS):
            qrow = r0 + qi
            w0 = min(max(qrow - WIN_H // 2, 0), ROWS - WIN_H)
            for kp in range(K_ROWS // 2):
                halves = []
                for krow in (k0 + 2 * kp, k0 + 2 * kp + 1):
                    in_win = w0 <= krow < w0 + WIN_H
                    halves.append(t_scr[krow - qrow + WIN_H - 1] if in_win else neg)
                o_ref[v, qi * GRID_W:(qi + 1) * GRID_W, kp * 2 * GRID_W:(kp + 1) * 2 * GRID_W] = (
                    jnp.where(left, halves[0], halves[1]))


def _bias_tables(rpb):
    return pl.pallas_call(
        _tbl_kernel,
        grid=(HEADS,),
        in_specs=[pl.BlockSpec(memory_space=pltpu.SMEM)],
        out_specs=pl.BlockSpec((len(TBL_R0), None, Q_BLK, K_BLK), lambda h: (0, h, 0, 0)),
        out_shape=jax.ShapeDtypeStruct((len(TBL_R0), HEADS, Q_BLK, K_BLK), F32),
        scratch_shapes=[pltpu.VMEM((N_DR, GRID_W, 2 * GRID_W), F32)],
        compiler_params=_params(("arbitrary",)),
    )(rpb.reshape(-1))


def _pin_after(value, anchor):
    bits = lambda t: pltpu.bitcast(t, jnp.uint32)
    zero = lax.shift_right_logical(lax.shift_right_logical(bits(anchor), jnp.uint32(16)), jnp.uint32(16))
    return pltpu.bitcast(bits(value) | zero, F32)


def _ffn_tail(x, y, mod_ref, ng2_ref, w_all, first_blk, fg_ref, side_work=None, h2_scr=None):
    x1 = x + mod_ref[:, 2 * D:3 * D] * y
    h2f = _rms_mod(x1, ng2_ref[...], mod_ref[:, 3 * D:4 * D], mod_ref[:, 4 * D:5 * D])
    h2 = h2f.astype(BF16)
    if side_work is not None:
        h2_scr[...] = h2
    acc = None
    token = None
    for c in range(FFN_CHUNKS):
        if side_work is not None:
            if token is not None:
                h2_scr[0:2 * SUBLANES, 0:LANES] = _pin_after(h2f[0:2 * SUBLANES, 0:LANES], token).astype(BF16)
                h2 = h2_scr[...]
            token = side_work(c, y if c == 0 else acc)
        hid = jnp.maximum(_dot(h2, w_all[first_blk + c]), 0.0)
        part = _dot((hid * hid).astype(BF16), w_all[first_blk + FFN_CHUNKS + c])
        acc = part if acc is None else acc + part
    x2 = x1 + mod_ref[:, 5 * D:6 * D] * acc
    if fg_ref is not None:
        x2 = x2 * lax.rsqrt(jnp.mean(x2 * x2, axis=-1, keepdims=True) + EPS) * fg_ref[...]
    return x2


IN_HBM = pl.BlockSpec(memory_space=pl.ANY)


def _ffn_blocks(w1_hbm, w2_hbm, layer):
    return ([w1_hbm.at[layer, :, pl.ds(c * WBLK, WBLK)] for c in range(FFN_CHUNKS)]
            + [w2_hbm.at[layer, pl.ds(c * WBLK, WBLK), :] for c in range(FFN_CHUNKS)])


def _stage_weights(srcs, w_all, stage, sem):
    copies = [pltpu.make_async_copy(src, stage.at[n % 2], sem.at[n % 2]) for n, src in enumerate(srcs)]
    copies[0].start()
    for n in range(len(srcs)):
        if n + 1 < len(srcs):
            copies[n + 1].start()
        copies[n].wait()
        w_all[n] = stage[n % 2].astype(BF16)


def _even_out_kernel(*refs, final, n_x, layer, j):
    x_refs, refs = refs[:n_x], refs[n_x:]
    a_ref, b_ref, mod_ref, ng2_ref, wo_hbm, w1_hbm, w2_hbm = refs[:7]
    fg_ref = refs[7] if final else None
    o_ref, w_all, stage, w_sem = refs[-4:]

    @pl.when(pl.program_id(0) == 0)
    def _():
        _stage_weights([wo_hbm.at[j]] + _ffn_blocks(w1_hbm, w2_hbm, layer), w_all, stage, w_sem)

    y = _dot(a_ref[...], w_all[0, 0:A_W, :]) + _dot(b_ref[...], w_all[0, A_W:A_W + B_W, :])
    o_ref[...] = _ffn_tail(_load_stream(x_refs, T_LAT // OUT_TILE), y, mod_ref, ng2_ref, w_all, 1, fg_ref)


def _even_out(a, b, xs, mod3, layer, ng2, w_out_all, j, w1_all, w2_all, n_out, final_g=None):
    x_specs, x_args = _stream_specs(xs, OUT_TILE)
    tok = lambda w: pl.BlockSpec((OUT_TILE, w), lambda i: (i, 0))
    in_specs = x_specs + [tok(A_W), tok(B_W), _mod_row_spec(layer, tiles_per_seq=SEQ // OUT_TILE),
                          _const_spec((1, D)), IN_HBM, IN_HBM, IN_HBM]
    args = x_args + [a, b, mod3, ng2, w_out_all, w1_all, w2_all]
    if final_g is not None:
        in_specs.append(_const_spec((1, D)))
        args.append(final_g)
    return pl.pallas_call(
        functools.partial(_even_out_kernel, final=final_g is not None, n_x=len(x_args), layer=layer, j=j),
        grid=(n_out // OUT_TILE,),
        in_specs=in_specs,
        out_specs=tok(D),
        out_shape=jax.ShapeDtypeStruct((n_out, D), F32),
        scratch_shapes=[pltpu.VMEM((1 + 2 * FFN_CHUNKS, WBLK, WBLK), BF16), pltpu.VMEM((2, WBLK, WBLK), F32),
                        pltpu.SemaphoreType.DMA((2,))],
        compiler_params=_params(("arbitrary",)),
    )(*args)


def _odd_kernel(*refs, final, n_tiles, layer, j):
    (xp_ref, xa_ref, xn_ref, xb_ref, moda_ref, modb_ref, ng1_ref, ng2_ref, wp1_hbm, bp1_ref, wdw_ref,
     bdw_ref, lg_ref, lb_ref, wp2_hbm, bp2_ref, w1_hbm, w2_hbm) = refs[:18]
    fg_ref = refs[18] if final else None
    o_ref, y_scr, conv_scr, w_scr, h2_scr, w_all, stage, w_sem = refs[-8:]
    i = pl.program_id(0)

    @pl.when(i == 0)
    def _():
        conv_scr[...] = jnp.zeros_like(conv_scr)
        _stage_weights([wp1_hbm.at[j, :, pl.ds(0, D)], wp1_hbm.at[j, :, pl.ds(D, D)], wp2_hbm.at[j]]
                       + _ffn_blocks(w1_hbm, w2_hbm, layer), w_all, stage, w_sem)

    ia = jnp.minimum(i, n_tiles - 1)
    is_lat = ia < LAT_TILES
    pos = ia % TILES_PER_SEQ
    seq_first = jnp.logical_or(jnp.logical_not(is_lat), pos == 0)
    seq_last = jnp.logical_or(jnp.logical_not(is_lat), pos == TILES_PER_SEQ - 1)
    xe = jnp.concatenate([xp_ref[...], xa_ref[...], xn_ref[...]], axis=0)
    h = _rms_mod(xe, ng1_ref[...], moda_ref[:, 0:D], moda_ref[:, D:2 * D]).astype(BF16)
    glu = (_dot(h, w_all[0]) + bp1_ref[:, 0:D]) * jax.nn.sigmoid(_dot(h, w_all[1]) + bp1_ref[:, D:2 * D])
    for cb in range(D // LANES):
        lanes = slice(cb * LANES, (cb + 1) * LANES)
        y_scr[cb, 0:HALO] = jnp.where(seq_first, 0.0, glu[0:HALO, lanes])
        y_scr[cb, HALO:HALO + TILE] = glu[HALO:HALO + TILE, lanes]
        y_scr[cb, HALO + TILE:] = jnp.where(seq_last, 0.0, glu[HALO + TILE:, lanes])

    acc = jnp.concatenate([conv_scr[cb] for cb in range(D // LANES)], axis=1)
    mu = jnp.mean(acc, axis=-1, keepdims=True)
    ac = acc - mu
    var = jnp.mean(ac * ac, axis=-1, keepdims=True)
    ln = ac * lax.rsqrt(var + EPS) * lg_ref[...] + lb_ref[...]
    act = (ln * jax.nn.sigmoid(ln)).astype(BF16)
    y = _dot(act, w_all[2]) + bp2_ref[...]

    base = HALO - CONV_K // 2
    rows_in = CONV_ROWS + SUBLANES

    row_blocks = TILE // CONV_ROWS

    def conv_piece(p, anchor):
        cb, r0 = p // row_blocks, (p % row_blocks) * CONV_ROWS
        lanes = slice(cb * LANES, (cb + 1) * LANES)
        w_scr[p, 0:CONV_K, :] = _pin_after(wdw_ref[:, lanes], anchor[0:1, 0:LANES])
        out = bdw_ref[:, lanes]
        for s in range(SUBLANES):
            part = None
            for off in range(s, base + CONV_K, SUBLANES):
                if off < base:
                    continue
                r1 = r0 + off - s
                term = y_scr[cb, r1:r1 + rows_in, :] * w_scr[p, off - base:off - base + 1, :]
                part = term if part is None else part + term
            out = out + part[s:s + CONV_ROWS]
        conv_scr[cb, r0:r0 + CONV_ROWS, :] = out
        return out[0:1, :]

    per_slot = CONV_PIECES // FFN_CHUNKS

    def conv_part(slot, anchor):
        for p in range(slot * per_slot, (slot + 1) * per_slot):
            token = conv_piece(p, anchor)
        return token

    o_ref[...] = _ffn_tail(xb_ref[...], y, modb_ref, ng2_ref, w_all, 3, fg_ref,
                           side_work=conv_part, h2_scr=h2_scr)


def _odd_layer(xs, mod3, layer, ng1, ng2, wp1_all, j, bp1, wdw, bdw, lg, lb, wp2_all, bp2, w1_all, w2_all,
               final_g=None):
    n = xs.shape[0]
    n_tiles = n // TILE
    per = TILE // HALO
    halo_blocks = n // HALO
    tile_a = lambda i: jnp.minimum(i, n_tiles - 1)
    tile_b = lambda i: jnp.maximum(i - 1, 0)
    in_specs = [
        pl.BlockSpec((HALO, D), lambda i: (jnp.maximum(tile_a(i) * per - 1, 0), 0)),
        pl.BlockSpec((TILE, D), lambda i: (tile_a(i), 0)),
        pl.BlockSpec((HALO, D), lambda i: (jnp.minimum((tile_a(i) + 1) * per, halo_blocks - 1), 0)),
        pl.BlockSpec((TILE, D), lambda i: (tile_b(i), 0)),
        _mod_row_spec(layer, tile_a), _mod_row_spec(layer, tile_b), _const_spec((1, D)), _const_spec((1, D)),
        IN_HBM, _const_spec((1, 2 * D)), _const_spec((CONV_K, D)), _const_spec((1, D)),
        _const_spec((1, D)), _const_spec((1, D)), IN_HBM, _const_spec((1, D)),
        IN_HBM, IN_HBM,
    ]
    args = [xs, xs, xs, xs, mod3, mod3, ng1, ng2, wp1_all, bp1, wdw, bdw, lg, lb, wp2_all, bp2, w1_all, w2_all]
    if final_g is not None:
        in_specs.append(_const_spec((1, D)))
        args.append(final_g)
    return pl.pallas_call(
        functools.partial(_odd_kernel, final=final_g is not None, n_tiles=n_tiles, layer=layer, j=j),
        grid=(n_tiles + 1,),
        in_specs=in_specs,
        out_specs=pl.BlockSpec((TILE, D), lambda i: (tile_b(i), 0)),
        out_shape=jax.ShapeDtypeStruct((n, D), F32),
        scratch_shapes=[pltpu.VMEM((D // LANES, HALO + TILE + HALO, LANES), F32),
                        pltpu.VMEM((D // LANES, TILE, LANES), F32),
                        pltpu.VMEM((CONV_PIECES, CONV_K + 1, LANES), F32), pltpu.VMEM((TILE, D), BF16),
                        pltpu.VMEM((3 + 2 * FFN_CHUNKS, WBLK, WBLK), BF16), pltpu.VMEM((2, WBLK, WBLK), F32),
                        pltpu.SemaphoreType.DMA((2,))],
        compiler_params=_params(("arbitrary",)),
    )(*args)


def kernel(x, c, ctx, c_ctx, w_mod, b_mod, norm_g, w_in, w_out, ln_v_g, w_sp, b_sp, rpb, w_pw1, b_pw1,
           w_dw, b_dw, ln_c_g, ln_c_b, w_pw2, b_pw2, w_ff1, w_ff2, final_g):
    row = lambda t: t.reshape(1, -1)
    cc = jnp.concatenate([c, c_ctx[None, :], jnp.zeros((MOD_ROWS - BATCH - 1, D), F32)], axis=0)
    mod3 = _modulation(cc, w_mod, b_mod).reshape(DEPTH * MOD_ROWS, 1, 6 * D)
    xs = (x.reshape(T_LAT, D), ctx.reshape(T_CTX, D))
    w_in, w_sp = w_in.astype(BF16), w_sp.astype(BF16)
    last_reader = ((DEPTH - 1) // 2) * 2
    for l in range(DEPTH):
        ctx_out = l < last_reader
        fg = row(final_g) if l == DEPTH - 1 else None
        j = l // 2
        if l % 2 == 0:
            a, q, k, v = _even_in(xs, mod3, l, row(norm_g[l, 0]), w_in, j, row(ln_v_g[j]),
                                  w_sp, b_sp[j][:, :, None])
            b = _attention(q, k, v, _bias_tables(rpb[j]), ctx_out)
            xs = _even_out(a, b, xs, mod3, l, row(norm_g[l, 1]), w_out, j, w_ff1, w_ff2,
                           T_ALL if ctx_out else T_LAT, fg)
        else:
            xs = _odd_layer(xs, mod3, l, row(norm_g[l, 0]), row(norm_g[l, 1]), w_pw1, j,
                            row(b_pw1[j]), w_dw[j], row(b_dw[j]), row(ln_c_g[j]), row(ln_c_b[j]),
                            w_pw2, row(b_pw2[j]), w_ff1, w_ff2, fg)
    return xs.reshape(BATCH, SEQ, D)
```

```python
import functools

import jax
import jax.numpy as jnp
from jax import lax
from jax.experimental import pallas as pl
from jax.experimental.pallas import tpu as pltpu

D = 1024
BATCH = 4
SEQ = 4096
DEPTH = 4
GRID_W = 64
ROWS = SEQ // GRID_W
CTX_LEN = 256
CHUNK = 128
A_GROUPS = 4
A_W = 512
B_W = 512
HEAD_DIM = 64
HEADS = 8
WIN_H = 8
WIN_W = 16
Q_ROWS = 4
Q_BLK = Q_ROWS * GRID_W
K_ROWS = Q_ROWS + WIN_H
K_BLK = K_ROWS * GRID_W
ROW_STEPS = ROWS // Q_ROWS
MIX_IN = 2 * A_W + 3 * B_W
CONV_K = 31
HALO = 16
SUBLANES = 8
LANES = 128
CONV_ROWS = 128
FFN = 4 * D
FFN_CHUNK = 1024
FFN_CHUNKS = FFN // FFN_CHUNK
WBLK = 1024
assert WBLK == D == FFN_CHUNK == A_W + B_W
EPS = 1e-6
NEG = -1e30
LOG2E = 1.4426950408889634

T_LAT = BATCH * SEQ
T_CTX = BATCH * CTX_LEN
T_ALL = T_LAT + T_CTX
TILE = 256
IN_TILE = 1024
OUT_TILE = 512
CONV_PIECES = (D // LANES) * (TILE // CONV_ROWS)
TILES_PER_SEQ = SEQ // TILE
LAT_TILES = T_LAT // TILE
MOD_ROWS = 8
MOD_N = 1536
VMEM_LIMIT = 56 * 1024 * 1024

BF16 = jnp.bfloat16
F32 = jnp.float32


def _const_spec(shape):
    zeros = (0,) * len(shape)
    return pl.BlockSpec(shape, lambda *_: zeros, pipeline_mode=pl.Buffered(1))


def _layer_spec(stacked, idx):
    shape = stacked.shape[1:]
    zeros = (0,) * len(shape)
    return pl.BlockSpec((None,) + shape, lambda *_: (idx,) + zeros, pipeline_mode=pl.Buffered(1))


def _params(sem):
    return pltpu.CompilerParams(dimension_semantics=sem, vmem_limit_bytes=VMEM_LIMIT)


def _mod_row_spec(layer, tile_of=lambda i: i, tiles_per_seq=TILES_PER_SEQ):
    return pl.BlockSpec(
        (None, 1, 6 * D),
        lambda i: (layer * MOD_ROWS + jnp.minimum(tile_of(i) // tiles_per_seq, BATCH), 0, 0))


def _rms_mod(x, g, shift, scale):
    return x * lax.rsqrt(jnp.mean(x * x, axis=-1, keepdims=True) + EPS) * (g * (1.0 + scale)) + shift


def _dot(a, b):
    return jnp.dot(a, b, preferred_element_type=F32)


def _dot_nt(a, b):
    return lax.dot_general(a, b, (((1,), (1,)), ((), ())), preferred_element_type=F32)


def _mod_kernel(cc_ref, w_ref, b_ref, o_ref):
    cc = cc_ref[...]
    sc = (cc * jax.nn.sigmoid(cc)).astype(BF16)
    o_ref[...] = _dot(sc, w_ref[...].astype(BF16)) + b_ref[...]


def _modulation(cc, w_mod, b_mod):
    nblk = (6 * D) // MOD_N
    return pl.pallas_call(
        _mod_kernel,
        grid=(DEPTH, nblk),
        in_specs=[
            pl.BlockSpec((MOD_ROWS, D), lambda l, n: (0, 0)),
            pl.BlockSpec((None, D, MOD_N), lambda l, n: (l, 0, n)),
            pl.BlockSpec((None, 1, MOD_N), lambda l, n: (l, 0, n)),
        ],
        out_specs=pl.BlockSpec((None, MOD_ROWS, MOD_N), lambda l, n: (l, 0, n)),
        out_shape=jax.ShapeDtypeStruct((DEPTH, MOD_ROWS, 6 * D), F32),
        compiler_params=_params(("arbitrary", "arbitrary")),
    )(cc, w_mod, b_mod.reshape(DEPTH, 1, 6 * D))


def _load_stream(x_refs, lat_tiles):
    if len(x_refs) == 1:
        return x_refs[0][...]
    return jnp.where(pl.program_id(0) < lat_tiles, x_refs[0][...], x_refs[1][...])


def _stream_specs(xs, tile):
    if not isinstance(xs, tuple):
        return [pl.BlockSpec((tile, D), lambda i: (i, 0))], [xs]
    lat_tiles = xs[0].shape[0] // tile
    return [pl.BlockSpec((tile, D), lambda i: (jnp.minimum(i, lat_tiles - 1), 0)),
            pl.BlockSpec((tile, D), lambda i: (jnp.maximum(i - lat_tiles, 0), 0))], list(xs)


def _even_in_kernel(*refs, n_x):
    x_refs = refs[:n_x]
    mod_ref, ng_ref, w_in_ref, lng_ref, w_sp_ref, b_sp_ref, a_ref, q_ref, k_ref, v_ref = refs[n_x:]
    x = _load_stream(x_refs, T_LAT // IN_TILE)
    h = _rms_mod(x, ng_ref[...], mod_ref[:, 0:D], mod_ref[:, D:2 * D]).astype(BF16)
    z = jax.nn.gelu(_dot(h, w_in_ref[:, 0:2 * A_W]))
    gd = A_W // A_GROUPS
    for grp in range(A_GROUPS):
        g = z[:, A_W + grp * gd:A_W + (grp + 1) * gd]
        mu = jnp.mean(g, axis=-1, keepdims=True)
        gc = g - mu
        var = jnp.mean(gc * gc, axis=-1, keepdims=True)
        gn = (gc * lax.rsqrt(var + EPS) * lng_ref[:, grp * gd:(grp + 1) * gd]).astype(BF16)
        for ch in range(IN_TILE // CHUNK):
            rows = slice(ch * CHUNK, (ch + 1) * CHUNK)
            s = _dot(w_sp_ref[grp], gn[rows]) + b_sp_ref[grp]
            a_ref[rows, grp * gd:(grp + 1) * gd] = (z[rows, grp * gd:(grp + 1) * gd] * s).astype(BF16)
    c0 = 2 * A_W
    q_ref[...] = (_dot(h, w_in_ref[:, c0:c0 + B_W]) * (HEAD_DIM ** -0.5 * LOG2E)).astype(BF16)
    k_ref[...] = _dot(h, w_in_ref[:, c0 + B_W:c0 + 2 * B_W]).astype(BF16)
    v_ref[...] = _dot(h, w_in_ref[:, c0 + 2 * B_W:c0 + 3 * B_W]).astype(BF16)


def _even_in(xs, mod3, layer, ng, w_in_all, j, lng, w_sp_all, b_sp):
    x_specs, x_args = _stream_specs(xs, IN_TILE)
    tok = lambda w: pl.BlockSpec((IN_TILE, w), lambda i: (i, 0))
    out = jax.ShapeDtypeStruct((T_ALL, A_W), BF16)
    return pl.pallas_call(
        functools.partial(_even_in_kernel, n_x=len(x_args)),
        grid=(T_ALL // IN_TILE,),
        in_specs=x_specs + [
            _mod_row_spec(layer, tiles_per_seq=SEQ // IN_TILE), _const_spec((1, D)), _layer_spec(w_in_all, j),
            _const_spec((1, A_W)), _layer_spec(w_sp_all, j), _const_spec((A_GROUPS, CHUNK, 1)),
        ],
        out_specs=[tok(A_W)] * 4,
        out_shape=[out] * 4,
        compiler_params=_params(("arbitrary",)),
    )(*x_args, mod3, ng, w_in_all, lng, w_sp_all, b_sp)


def _attn_kernel(q_ref, k_ref, v_ref, kc_ref, vc_ref, tbl_ref, o_ref, *, ctx_queries):
    r = pl.program_id(1)
    lane = lax.broadcasted_iota(jnp.int32, (Q_BLK, 2 * HEAD_DIM), 1)
    first_head = lane < HEAD_DIM

    def attend(local):
        if local:
            start = pl.multiple_of(
                jnp.clip(r * Q_ROWS - WIN_H // 2, 0, ROWS - K_ROWS) * GRID_W, GRID_W)
        for hp in range(HEADS // 2):
            cols = slice(hp * 2 * HEAD_DIM, (hp + 1) * 2 * HEAD_DIM)
            q2 = q_ref[:, cols]
            k2 = kc_ref[:, cols]
            v2 = vc_ref[:, cols]
            if local:
                k2 = jnp.concatenate([k_ref[pl.ds(start, K_BLK), cols], k2], axis=0)
                v2 = jnp.concatenate([v_ref[pl.ds(start, K_BLK), cols], v2], axis=0)
            zero = jnp.zeros_like(q2)
            s2 = _dot_nt(jnp.concatenate([jnp.where(first_head, q2, zero), jnp.where(first_head, zero, q2)],
                                         axis=0), k2)
            ps, dens = [], []
            for half in range(2):
                s = s2[half * Q_BLK:(half + 1) * Q_BLK]
                if local:
                    s = jnp.concatenate([s[:, 0:K_BLK] + tbl_ref[2 * hp + half], s[:, K_BLK:]], axis=1)
                m = jnp.max(s, axis=-1, keepdims=True)
                p = jnp.exp2(s - m)
                dens.append(jnp.sum(p, axis=-1, keepdims=True))
                ps.append(p.astype(BF16))
            o2 = _dot(jnp.concatenate(ps, axis=0), v2)
            o_ref[:, cols] = jnp.where(first_head, o2[0:Q_BLK] / dens[0], o2[Q_BLK:] / dens[1]).astype(BF16)

    if ctx_queries:
        pl.when(r < ROW_STEPS)(lambda: attend(True))
        pl.when(r >= ROW_STEPS)(lambda: attend(False))
    else:
        attend(True)


def _attention(q, k, v, tbl, ctx_queries):
    steps = ROW_STEPS + (1 if ctx_queries else 0)
    lat_blocks = T_LAT // Q_BLK

    def q_idx(b, r):
        return (jnp.where(r < ROW_STEPS, b * ROW_STEPS + r, lat_blocks + b), 0)

    def tbl_idx(b, r):
        return (jnp.where(r == 0, 0, jnp.where(r >= ROW_STEPS - 1, 2, 1)), 0, 0, 0)

    ctx_blk = T_LAT // CTX_LEN
    return pl.pallas_call(
        functools.partial(_attn_kernel, ctx_queries=ctx_queries),
        grid=(BATCH, steps),
        in_specs=[
            pl.BlockSpec((Q_BLK, B_W), q_idx),
            pl.BlockSpec((SEQ, B_W), lambda b, r: (b, 0)),
            pl.BlockSpec((SEQ, B_W), lambda b, r: (b, 0)),
            pl.BlockSpec((CTX_LEN, B_W), lambda b, r: (ctx_blk + b, 0)),
            pl.BlockSpec((CTX_LEN, B_W), lambda b, r: (ctx_blk + b, 0)),
            pl.BlockSpec((None, HEADS, Q_BLK, K_BLK), tbl_idx),
        ],
        out_specs=pl.BlockSpec((Q_BLK, B_W), q_idx),
        out_shape=jax.ShapeDtypeStruct((T_ALL if ctx_queries else T_LAT, B_W), BF16),
        compiler_params=_params(("arbitrary", "arbitrary")),
    )(q, k, v, k, v, tbl)


N_DR = 2 * WIN_H - 1
N_DC = 2 * WIN_W - 1
TBL_R0 = (0, Q_ROWS, ROWS - Q_ROWS)


def _tbl_kernel(rpb_ref, o_ref, t_scr):
    h = pl.program_id(0)
    shape = (GRID_W, 2 * GRID_W)
    cq = lax.broadcasted_iota(jnp.int32, shape, 0)
    lane = lax.broadcasted_iota(jnp.int32, shape, 1)
    ck = lane & (GRID_W - 1)
    dc = jnp.clip(ck - cq + WIN_W - 1, 0, N_DC - 1)
    cs = jnp.clip(cq - WIN_W // 2, 0, GRID_W - WIN_W)
    col_ok = (ck >= cs) & (ck < cs + WIN_W)
    neg = jnp.full(shape, NEG, F32)
    for dr in range(N_DR):
        t = neg
        for j in range(N_DC):
            t = jnp.where(dc == j, rpb_ref[(h * N_DR + dr) * N_DC + j], t)
        t_scr[dr] = jnp.where(col_ok, t * LOG2E, NEG)
    left = lane < GRID_W
    for v, r0 in enumerate(TBL_R0):
        k0 = min(max(r0 - WIN_H // 2, 0), ROWS - K_ROWS)
        for qi in range(Q_ROWS):
            qrow = r0 + qi
            w0 = min(max(qrow - WIN_H // 2, 0), ROWS - WIN_H)
            for kp in range(K_ROWS // 2):
                halves = []
                for krow in (k0 + 2 * kp, k0 + 2 * kp + 1):
                    in_win = w0 <= krow < w0 + WIN_H
                    halves.append(t_scr[krow - qrow + WIN_H - 1] if in_win else neg)
                o_ref[v, qi * GRID_W:(qi + 1) * GRID_W, kp * 2 * GRID_W:(kp + 1) * 2 * GRID_W] = (
                    jnp.where(left, halves[0], halves[1]))


def _bias_tables(rpb):
    return pl.pallas_call(
        _tbl_kernel,
        grid=(HEADS,),
        in_specs=[pl.BlockSpec(memory_space=pltpu.SMEM)],
        out_specs=pl.BlockSpec((len(TBL_R0), None, Q_BLK, K_BLK), lambda h: (0, h, 0, 0)),
        out_shape=jax.ShapeDtypeStruct((len(TBL_R0), HEADS, Q_BLK, K_BLK), F32),
        scratch_shapes=[pltpu.VMEM((N_DR, GRID_W, 2 * GRID_W), F32)],
        compiler_params=_params(("arbitrary",)),
    )(rpb.reshape(-1))


def _pin_after(value, anchor):
    bits = lambda t: pltpu.bitcast(t, jnp.uint32)
    zero = lax.shift_right_logical(lax.shift_right_logical(bits(anchor), jnp.uint32(16)), jnp.uint32(16))
    return pltpu.bitcast(bits(value) | zero, F32)


def _ffn_tail(x, y, mod_ref, ng2_ref, w_all, first_blk, fg_ref, side_work=None, h2_scr=None):
    x1 = x + mod_ref[:, 2 * D:3 * D] * y
    h2f = _rms_mod(x1, ng2_ref[...], mod_ref[:, 3 * D:4 * D], mod_ref[:, 4 * D:5 * D])
    h2 = h2f.astype(BF16)
    if side_work is not None:
        h2_scr[...] = h2
    acc = None
    token = None
    for c in range(FFN_CHUNKS):
        if side_work is not None:
            if token is not None:
                h2_scr[0:2 * SUBLANES, 0:LANES] = _pin_after(h2f[0:2 * SUBLANES, 0:LANES], token).astype(BF16)
                h2 = h2_scr[...]
            token = side_work(c, y if c == 0 else acc)
        hid = jnp.maximum(_dot(h2, w_all[first_blk + c]), 0.0)
        part = _dot((hid * hid).astype(BF16), w_all[first_blk + FFN_CHUNKS + c])
        acc = part if acc is None else acc + part
    x2 = x1 + mod_ref[:, 5 * D:6 * D] * acc
    if fg_ref is not None:
        x2 = x2 * lax.rsqrt(jnp.mean(x2 * x2, axis=-1, keepdims=True) + EPS) * fg_ref[...]
    return x2


IN_HBM = pl.BlockSpec(memory_space=pl.ANY)


def _ffn_blocks(w1_hbm, w2_hbm, layer):
    return ([w1_hbm.at[layer, :, pl.ds(c * WBLK, WBLK)] for c in range(FFN_CHUNKS)]
            + [w2_hbm.at[layer, pl.ds(c * WBLK, WBLK), :] for c in range(FFN_CHUNKS)])


def _stage_weights(srcs, w_all, stage, sem):
    copies = [pltpu.make_async_copy(src, stage.at[n % 2], sem.at[n % 2]) for n, src in enumerate(srcs)]
    copies[0].start()
    for n in range(len(srcs)):
        if n + 1 < len(srcs):
            copies[n + 1].start()
        copies[n].wait()
        w_all[n] = stage[n % 2].astype(BF16)


def _even_out_kernel(*refs, final, n_x, layer, j):
    x_refs, refs = refs[:n_x], refs[n_x:]
    a_ref, b_ref, mod_ref, ng2_ref, wo_hbm, w1_hbm, w2_hbm = refs[:7]
    fg_ref = refs[7] if final else None
    o_ref, w_all, stage, w_sem = refs[-4:]

    @pl.when(pl.program_id(0) == 0)
    def _():
        _stage_weights([wo_hbm.at[j]] + _ffn_blocks(w1_hbm, w2_hbm, layer), w_all, stage, w_sem)

    y = _dot(a_ref[...], w_all[0, 0:A_W, :]) + _dot(b_ref[...], w_all[0, A_W:A_W + B_W, :])
    o_ref[...] = _ffn_tail(_load_stream(x_refs, T_LAT // OUT_TILE), y, mod_ref, ng2_ref, w_all, 1, fg_ref)


def _even_out(a, b, xs, mod3, layer, ng2, w_out_all, j, w1_all, w2_all, n_out, final_g=None):
    x_specs, x_args = _stream_specs(xs, OUT_TILE)
    tok = lambda w: pl.BlockSpec((OUT_TILE, w), lambda i: (i, 0))
    in_specs = x_specs + [tok(A_W), tok(B_W), _mod_row_spec(layer, tiles_per_seq=SEQ // OUT_TILE),
                          _const_spec((1, D)), IN_HBM, IN_HBM, IN_HBM]
    args = x_args + [a, b, mod3, ng2, w_out_all, w1_all, w2_all]
    if final_g is not None:
        in_specs.append(_const_spec((1, D)))
        args.append(final_g)
    return pl.pallas_call(
        functools.partial(_even_out_kernel, final=final_g is not None, n_x=len(x_args), layer=layer, j=j),
        grid=(n_out // OUT_TILE,),
        in_specs=in_specs,
        out_specs=tok(D),
        out_shape=jax.ShapeDtypeStruct((n_out, D), F32),
        scratch_shapes=[pltpu.VMEM((1 + 2 * FFN_CHUNKS, WBLK, WBLK), BF16), pltpu.VMEM((2, WBLK, WBLK), F32),
                        pltpu.SemaphoreType.DMA((2,))],
        compiler_params=_params(("arbitrary",)),
    )(*args)


def _odd_kernel(*refs, final, n_tiles, layer, j):
    (xp_ref, xa_ref, xn_ref, xb_ref, moda_ref, modb_ref, ng1_ref, ng2_ref, wp1_hbm, bp1_ref, wdw_ref,
     bdw_ref, lg_ref, lb_ref, wp2_hbm, bp2_ref, w1_hbm, w2_hbm) = refs[:18]
    fg_ref = refs[18] if final else None
    o_ref, y_scr, conv_scr, w_scr, h2_scr, w_all, stage, w_sem = refs[-8:]
    i = pl.program_id(0)

    @pl.when(i == 0)
    def _():
        conv_scr[...] = jnp.zeros_like(conv_scr)
        _stage_weights([wp1_hbm.at[j, :, pl.ds(0, D)], wp1_hbm.at[j, :, pl.ds(D, D)], wp2_hbm.at[j]]
                       + _ffn_blocks(w1_hbm, w2_hbm, layer), w_all, stage, w_sem)

    ia = jnp.minimum(i, n_tiles - 1)
    is_lat = ia < LAT_TILES
    pos = ia % TILES_PER_SEQ
    seq_first = jnp.logical_or(jnp.logical_not(is_lat), pos == 0)
    seq_last = jnp.logical_or(jnp.logical_not(is_lat), pos == TILES_PER_SEQ - 1)
    xe = jnp.concatenate([xp_ref[...], xa_ref[...], xn_ref[...]], axis=0)
    h = _rms_mod(xe, ng1_ref[...], moda_ref[:, 0:D], moda_ref[:, D:2 * D]).astype(BF16)
    glu = (_dot(h, w_all[0]) + bp1_ref[:, 0:D]) * jax.nn.sigmoid(_dot(h, w_all[1]) + bp1_ref[:, D:2 * D])
    for cb in range(D // LANES):
        lanes = slice(cb * LANES, (cb + 1) * LANES)
        y_scr[cb, 0:HALO] = jnp.where(seq_first, 0.0, glu[0:HALO, lanes])
        y_scr[cb, HALO:HALO + TILE] = glu[HALO:HALO + TILE, lanes]
        y_scr[cb, HALO + TILE:] = jnp.where(seq_last, 0.0, glu[HALO + TILE:, lanes])

    acc = jnp.concatenate([conv_scr[cb] for cb in range(D // LANES)], axis=1)
    mu = jnp.mean(acc, axis=-1, keepdims=True)
    ac = acc - mu
    var = jnp.mean(ac * ac, axis=-1, keepdims=True)
    ln = ac * lax.rsqrt(var + EPS) * lg_ref[...] + lb_ref[...]
    act = (ln * jax.nn.sigmoid(ln)).astype(BF16)
    y = _dot(act, w_all[2]) + bp2_ref[...]

    base = HALO - CONV_K // 2
    rows_in = CONV_ROWS + SUBLANES

    row_blocks = TILE // CONV_ROWS

    def conv_piece(p, anchor):
        cb, r0 = p // row_blocks, (p % row_blocks) * CONV_ROWS
        lanes = slice(cb * LANES, (cb + 1) * LANES)
        w_scr[p, 0:CONV_K, :] = _pin_after(wdw_ref[:, lanes], anchor[0:1, 0:LANES])
        out = bdw_ref[:, lanes]
        for s in range(SUBLANES):
            part = None
            for off in range(s, base + CONV_K, SUBLANES):
                if off < base:
                    continue
                r1 = r0 + off - s
                term = y_scr[cb, r1:r1 + rows_in, :] * w_scr[p, off - base:off - base + 1, :]
                part = term if part is None else part + term
            out = out + part[s:s + CONV_ROWS]
        conv_scr[cb, r0:r0 + CONV_ROWS, :] = out
        return out[0:1, :]

    per_slot = CONV_PIECES // FFN_CHUNKS

    def conv_part(slot, anchor):
        for p in range(slot * per_slot, (slot + 1) * per_slot):
            token = conv_piece(p, anchor)
        return token

    o_ref[...] = _ffn_tail(xb_ref[...], y, modb_ref, ng2_ref, w_all, 3, fg_ref,
                           side_work=conv_part, h2_scr=h2_scr)


def _odd_layer(xs, mod3, layer, ng1, ng2, wp1_all, j, bp1, wdw, bdw, lg, lb, wp2_all, bp2, w1_all, w2_all,
               final_g=None):
    n = xs.shape[0]
    n_tiles = n // TILE
    per = TILE // HALO
    halo_blocks = n // HALO
    tile_a = lambda i: jnp.minimum(i, n_tiles - 1)
    tile_b = lambda i: jnp.maximum(i - 1, 0)
    in_specs = [
        pl.BlockSpec((HALO, D), lambda i: (jnp.maximum(tile_a(i) * per - 1, 0), 0)),
        pl.BlockSpec((TILE, D), lambda i: (tile_a(i), 0)),
        pl.BlockSpec((HALO, D), lambda i: (jnp.minimum((tile_a(i) + 1) * per, halo_blocks - 1), 0)),
        pl.BlockSpec((TILE, D), lambda i: (tile_b(i), 0)),
        _mod_row_spec(layer, tile_a), _mod_row_spec(layer, tile_b), _const_spec((1, D)), _const_spec((1, D)),
        IN_HBM, _const_spec((1, 2 * D)), _const_spec((CONV_K, D)), _const_spec((1, D)),
        _const_spec((1, D)), _const_spec((1, D)), IN_HBM, _const_spec((1, D)),
        IN_HBM, IN_HBM,
    ]
    args = [xs, xs, xs, xs, mod3, mod3, ng1, ng2, wp1_all, bp1, wdw, bdw, lg, lb, wp2_all, bp2, w1_all, w2_all]
    if final_g is not None:
        in_specs.append(_const_spec((1, D)))
        args.append(final_g)
    return pl.pallas_call(
        functools.partial(_odd_kernel, final=final_g is not None, n_tiles=n_tiles, layer=layer, j=j),
        grid=(n_tiles + 1,),
        in_specs=in_specs,
        out_specs=pl.BlockSpec((TILE, D), lambda i: (tile_b(i), 0)),
        out_shape=jax.ShapeDtypeStruct((n, D), F32),
        scratch_shapes=[pltpu.VMEM((D // LANES, HALO + TILE + HALO, LANES), F32),
                        pltpu.VMEM((D // LANES, TILE, LANES), F32),
                        pltpu.VMEM((CONV_PIECES, CONV_K + 1, LANES), F32), pltpu.VMEM((TILE, D), BF16),
                        pltpu.VMEM((3 + 2 * FFN_CHUNKS, WBLK, WBLK), BF16), pltpu.VMEM((2, WBLK, WBLK), F32),
                        pltpu.SemaphoreType.DMA((2,))],
        compiler_params=_params(("arbitrary",)),
    )(*args)


def kernel(x, c, ctx, c_ctx, w_mod, b_mod, norm_g, w_in, w_out, ln_v_g, w_sp, b_sp, rpb, w_pw1, b_pw1,
           w_dw, b_dw, ln_c_g, ln_c_b, w_pw2, b_pw2, w_ff1, w_ff2, final_g):
    row = lambda t: t.reshape(1, -1)
    cc = jnp.concatenate([c, c_ctx[None, :], jnp.zeros((MOD_ROWS - BATCH - 1, D), F32)], axis=0)
    mod3 = _modulation(cc, w_mod, b_mod).reshape(DEPTH * MOD_ROWS, 1, 6 * D)
    xs = (x.reshape(T_LAT, D), ctx.reshape(T_CTX, D))
    w_in, w_sp = w_in.astype(BF16), w_sp.astype(BF16)
    last_reader = ((DEPTH - 1) // 2) * 2
    for l in range(DEPTH):
        ctx_out = l < last_reader
        fg = row(final_g) if l == DEPTH - 1 else None
        j = l // 2
        if l % 2 == 0:
            a, q, k, v = _even_in(xs, mod3, l, row(norm_g[l, 0]), w_in, j, row(ln_v_g[j]),
                                  w_sp, b_sp[j][:, :, None])
            b = _attention(q, k, v, _bias_tables(rpb[j]), ctx_out)
            xs = _even_out(a, b, xs, mod3, l, row(norm_g[l, 1]), w_out, j, w_ff1, w_ff2,
                           T_ALL if ctx_out else T_LAT, fg)
        else:
            xs = _odd_layer(xs, mod3, l, row(norm_g[l, 0]), row(norm_g[l, 1]), w_pw1, j,
                            row(b_pw1[j]), w_dw[j], row(b_dw[j]), row(ln_c_g[j]), row(ln_c_b[j]),
                            w_pw2, row(b_pw2[j]), w_ff1, w_ff2, fg)
    return xs.reshape(BATCH, SEQ, D)
```

```python
import functools

import jax
import jax.numpy as jnp
from jax import lax
from jax.experimental import pallas as pl
from jax.experimental.pallas import tpu as pltpu

D = 1024
BATCH = 4
SEQ = 4096
DEPTH = 4
GRID_W = 64
ROWS = SEQ // GRID_W
CTX_LEN = 256
CHUNK = 128
A_GROUPS = 4
A_W = 512
B_W = 512
HEAD_DIM = 64
HEADS = 8
WIN_H = 8
WIN_W = 16
Q_ROWS = 4
Q_BLK = Q_ROWS * GRID_W
K_ROWS = Q_ROWS + WIN_H
K_BLK = K_ROWS * GRID_W
ROW_STEPS = ROWS // Q_ROWS
MIX_IN = 2 * A_W + 3 * B_W
CONV_K = 31
HALO = 16
SUBLANES = 8
LANES = 128
CONV_ROWS = 128
FFN = 4 * D
FFN_CHUNK = 1024
FFN_CHUNKS = FFN // FFN_CHUNK
WBLK = 1024
assert WBLK == D == FFN_CHUNK == A_W + B_W
EPS = 1e-6
NEG = -1e30
LOG2E = 1.4426950408889634

T_LAT = BATCH * SEQ
T_CTX = BATCH * CTX_LEN
T_ALL = T_LAT + T_CTX
TILE = 256
IN_TILE = 1024
OUT_TILE = 512
CONV_PIECES = (D // LANES) * (TILE // CONV_ROWS)
TILES_PER_SEQ = SEQ // TILE
LAT_TILES = T_LAT // TILE
MOD_ROWS = 8
MOD_N = 1536
VMEM_LIMIT = 56 * 1024 * 1024

BF16 = jnp.bfloat16
F32 = jnp.float32


def _const_spec(shape):
    zeros = (0,) * len(shape)
    return pl.BlockSpec(shape, lambda *_: zeros, pipeline_mode=pl.Buffered(1))


def _layer_spec(stacked, idx):
    shape = stacked.shape[1:]
    zeros = (0,) * len(shape)
    return pl.BlockSpec((None,) + shape, lambda *_: (idx,) + zeros, pipeline_mode=pl.Buffered(1))


def _params(sem):
    return pltpu.CompilerParams(dimension_semantics=sem, vmem_limit_bytes=VMEM_LIMIT)


def _mod_row_spec(layer, tile_of=lambda i: i, tiles_per_seq=TILES_PER_SEQ):
    return pl.BlockSpec(
        (None, 1, 6 * D),
        lambda i: (layer * MOD_ROWS + jnp.minimum(tile_of(i) // tiles_per_seq, BATCH), 0, 0))


def _rms_mod(x, g, shift, scale):
    return x * lax.rsqrt(jnp.mean(x * x, axis=-1, keepdims=True) + EPS) * (g * (1.0 + scale)) + shift


def _dot(a, b):
    return jnp.dot(a, b, preferred_element_type=F32)


def _dot_nt(a, b):
    return lax.dot_general(a, b, (((1,), (1,)), ((), ())), preferred_element_type=F32)


def _mod_kernel(cc_ref, w_ref, b_ref, o_ref):
    cc = cc_ref[...]
    sc = (cc * jax.nn.sigmoid(cc)).astype(BF16)
    o_ref[...] = _dot(sc, w_ref[...].astype(BF16)) + b_ref[...]


def _modulation(cc, w_mod, b_mod):
    nblk = (6 * D) // MOD_N
    return pl.pallas_call(
        _mod_kernel,
        grid=(DEPTH, nblk),
        in_specs=[
            pl.BlockSpec((MOD_ROWS, D), lambda l, n: (0, 0)),
            pl.BlockSpec((None, D, MOD_N), lambda l, n: (l, 0, n)),
            pl.BlockSpec((None, 1, MOD_N), lambda l, n: (l, 0, n)),
        ],
        out_specs=pl.BlockSpec((None, MOD_ROWS, MOD_N), lambda l, n: (l, 0, n)),
        out_shape=jax.ShapeDtypeStruct((DEPTH, MOD_ROWS, 6 * D), F32),
        compiler_params=_params(("arbitrary", "arbitrary")),
    )(cc, w_mod, b_mod.reshape(DEPTH, 1, 6 * D))


def _load_stream(x_refs, lat_tiles):
    if len(x_refs) == 1:
        return x_refs[0][...]
    return jnp.where(pl.program_id(0) < lat_tiles, x_refs[0][...], x_refs[1][...])


def _stream_specs(xs, tile):
    if not isinstance(xs, tuple):
        return [pl.BlockSpec((tile, D), lambda i: (i, 0))], [xs]
    lat_tiles = xs[0].shape[0] // tile
    return [pl.BlockSpec((tile, D), lambda i: (jnp.minimum(i, lat_tiles - 1), 0)),
            pl.BlockSpec((tile, D), lambda i: (jnp.maximum(i - lat_tiles, 0), 0))], list(xs)


def _even_in_kernel(*refs, n_x):
    x_refs = refs[:n_x]
    mod_ref, ng_ref, w_in_ref, lng_ref, w_sp_ref, b_sp_ref, a_ref, q_ref, k_ref, v_ref = refs[n_x:]
    x = _load_stream(x_refs, T_LAT // IN_TILE)
    h = _rms_mod(x, ng_ref[...], mod_ref[:, 0:D], mod_ref[:, D:2 * D]).astype(BF16)
    zp = _dot(h, w_in_ref[:, 0:2 * A_W])
    c0 = 2 * A_W
    q_ref[...] = (_dot(h, w_in_ref[:, c0:c0 + B_W]) * (HEAD_DIM ** -0.5 * LOG2E)).astype(BF16)
    k_ref[...] = _dot(h, w_in_ref[:, c0 + B_W:c0 + 2 * B_W]).astype(BF16)
    v_ref[...] = _dot(h, w_in_ref[:, c0 + 2 * B_W:c0 + 3 * B_W]).astype(BF16)
    z = jax.nn.gelu(zp)
    gd = A_W // A_GROUPS
    for grp in range(A_GROUPS):
        g = z[:, A_W + grp * gd:A_W + (grp + 1) * gd]
        mu = jnp.mean(g, axis=-1, keepdims=True)
        gc = g - mu
        var = jnp.mean(gc * gc, axis=-1, keepdims=True)
        gn = (gc * lax.rsqrt(var + EPS) * lng_ref[:, grp * gd:(grp + 1) * gd]).astype(BF16)
        for ch in range(IN_TILE // CHUNK):
            rows = slice(ch * CHUNK, (ch + 1) * CHUNK)
            s = _dot(w_sp_ref[grp], gn[rows]) + b_sp_ref[grp]
            a_ref[rows, grp * gd:(grp + 1) * gd] = (z[rows, grp * gd:(grp + 1) * gd] * s).astype(BF16)


def _even_in(xs, mod3, layer, ng, w_in_all, j, lng, w_sp_all, b_sp):
    x_specs, x_args = _stream_specs(xs, IN_TILE)
    tok = lambda w: pl.BlockSpec((IN_TILE, w), lambda i: (i, 0))
    out = jax.ShapeDtypeStruct((T_ALL, A_W), BF16)
    return pl.pallas_call(
        functools.partial(_even_in_kernel, n_x=len(x_args)),
        grid=(T_ALL // IN_TILE,),
        in_specs=x_specs + [
            _mod_row_spec(layer, tiles_per_seq=SEQ // IN_TILE), _const_spec((1, D)), _layer_spec(w_in_all, j),
            _const_spec((1, A_W)), _layer_spec(w_sp_all, j), _const_spec((A_GROUPS, CHUNK, 1)),
        ],
        out_specs=[tok(A_W)] * 4,
        out_shape=[out] * 4,
        compiler_params=_params(("arbitrary",)),
    )(*x_args, mod3, ng, w_in_all, lng, w_sp_all, b_sp)


def _attn_kernel(q_ref, k_ref, v_ref, kc_ref, vc_ref, tbl_ref, o_ref, *, ctx_queries):
    r = pl.program_id(1)
    lane = lax.broadcasted_iota(jnp.int32, (Q_BLK, 2 * HEAD_DIM), 1)
    first_head = lane < HEAD_DIM

    def attend(local):
        if local:
            start = pl.multiple_of(
                jnp.clip(r * Q_ROWS - WIN_H // 2, 0, ROWS - K_ROWS) * GRID_W, GRID_W)

        def pair_scores(hp):
            cols = slice(hp * 2 * HEAD_DIM, (hp + 1) * 2 * HEAD_DIM)
            q2 = q_ref[:, cols]
            k2 = kc_ref[:, cols]
            if local:
                k2 = jnp.concatenate([k_ref[pl.ds(start, K_BLK), cols], k2], axis=0)
            zero = jnp.zeros_like(q2)
            return _dot_nt(jnp.concatenate([jnp.where(first_head, q2, zero), jnp.where(first_head, zero, q2)],
                                           axis=0), k2)

        def pair_output(hp, s2):
            cols = slice(hp * 2 * HEAD_DIM, (hp + 1) * 2 * HEAD_DIM)
            v2 = vc_ref[:, cols]
            if local:
                v2 = jnp.concatenate([v_ref[pl.ds(start, K_BLK), cols], v2], axis=0)
            ps, dens = [], []
            for half in range(2):
                s = s2[half * Q_BLK:(half + 1) * Q_BLK]
                if local:
                    s = jnp.concatenate([s[:, 0:K_BLK] + tbl_ref[2 * hp + half], s[:, K_BLK:]], axis=1)
                m = jnp.max(s, axis=-1, keepdims=True)
                p = jnp.exp2(s - m)
                dens.append(jnp.sum(p, axis=-1, keepdims=True))
                ps.append(p.astype(BF16))
            o2 = _dot(jnp.concatenate(ps, axis=0), v2)
            o_ref[:, cols] = jnp.where(first_head, o2[0:Q_BLK] / dens[0], o2[Q_BLK:] / dens[1]).astype(BF16)

        s2 = pair_scores(0)
        for hp in range(HEADS // 2):
            s2_next = pair_scores(hp + 1) if hp + 1 < HEADS // 2 else None
            pair_output(hp, s2)
            s2 = s2_next

    if ctx_queries:
        pl.when(r < ROW_STEPS)(lambda: attend(True))
        pl.when(r >= ROW_STEPS)(lambda: attend(False))
    else:
        attend(True)


def _attention(q, k, v, tbl, ctx_queries):
    steps = ROW_STEPS + (1 if ctx_queries else 0)
    lat_blocks = T_LAT // Q_BLK

    def q_idx(b, r):
        return (jnp.where(r < ROW_STEPS, b * ROW_STEPS + r, lat_blocks + b), 0)

    def tbl_idx(b, r):
        return (jnp.where(r == 0, 0, jnp.where(r >= ROW_STEPS - 1, 2, 1)), 0, 0, 0)

    ctx_blk = T_LAT // CTX_LEN
    return pl.pallas_call(
        functools.partial(_attn_kernel, ctx_queries=ctx_queries),
        grid=(BATCH, steps),
        in_specs=[
            pl.BlockSpec((Q_BLK, B_W), q_idx),
            pl.BlockSpec((SEQ, B_W), lambda b, r: (b, 0)),
            pl.BlockSpec((SEQ, B_W), lambda b, r: (b, 0)),
            pl.BlockSpec((CTX_LEN, B_W), lambda b, r: (ctx_blk + b, 0)),
            pl.BlockSpec((CTX_LEN, B_W), lambda b, r: (ctx_blk + b, 0)),
            pl.BlockSpec((None, HEADS, Q_BLK, K_BLK), tbl_idx),
        ],
        out_specs=pl.BlockSpec((Q_BLK, B_W), q_idx),
        out_shape=jax.ShapeDtypeStruct((T_ALL if ctx_queries else T_LAT, B_W), BF16),
        compiler_params=_params(("arbitrary", "arbitrary")),
    )(q, k, v, k, v, tbl)


N_DR = 2 * WIN_H - 1
N_DC = 2 * WIN_W - 1
TBL_R0 = (0, Q_ROWS, ROWS - Q_ROWS)


def _tbl_kernel(rpb_ref, o_ref, t_scr):
    h = pl.program_id(0)
    shape = (GRID_W, 2 * GRID_W)
    cq = lax.broadcasted_iota(jnp.int32, shape, 0)
    lane = lax.broadcasted_iota(jnp.int32, shape, 1)
    ck = lane & (GRID_W - 1)
    dc = jnp.clip(ck - cq + WIN_W - 1, 0, N_DC - 1)
    cs = jnp.clip(cq - WIN_W // 2, 0, GRID_W - WIN_W)
    col_ok = (ck >= cs) & (ck < cs + WIN_W)
    neg = jnp.full(shape, NEG, F32)
    for dr in range(N_DR):
        t = neg
        for j in range(N_DC):
            t = jnp.where(dc == j, rpb_ref[(h * N_DR + dr) * N_DC + j], t)
        t_scr[dr] = jnp.where(col_ok, t * LOG2E, NEG)
    left = lane < GRID_W
    for v, r0 in enumerate(TBL_R0):
        k0 = min(max(r0 - WIN_H // 2, 0), ROWS - K_ROWS)
        for qi in range(Q_ROWS):
            qrow = r0 + qi
            w0 = min(max(qrow - WIN_H // 2, 0), ROWS - WIN_H)
            for kp in range(K_ROWS // 2):
                halves = []
                for krow in (k0 + 2 * kp, k0 + 2 * kp + 1):
                    in_win = w0 <= krow < w0 + WIN_H
                    halves.append(t_scr[krow - qrow + WIN_H - 1] if in_win else neg)
                o_ref[v, qi * GRID_W:(qi + 1) * GRID_W, kp * 2 * GRID_W:(kp + 1) * 2 * GRID_W] = (
                    jnp.where(left, halves[0], halves[1]))


def _bias_tables(rpb):
    return pl.pallas_call(
        _tbl_kernel,
        grid=(HEADS,),
        in_specs=[pl.BlockSpec(memory_space=pltpu.SMEM)],
        out_specs=pl.BlockSpec((len(TBL_R0), None, Q_BLK, K_BLK), lambda h: (0, h, 0, 0)),
        out_shape=jax.ShapeDtypeStruct((len(TBL_R0), HEADS, Q_BLK, K_BLK), F32),
        scratch_shapes=[pltpu.VMEM((N_DR, GRID_W, 2 * GRID_W), F32)],
        compiler_params=_params(("arbitrary",)),
    )(rpb.reshape(-1))


def _pin_after(value, anchor):
    bits = lambda t: pltpu.bitcast(t, jnp.uint32)
    zero = lax.shift_right_logical(lax.shift_right_logical(bits(anchor), jnp.uint32(16)), jnp.uint32(16))
    return pltpu.bitcast(bits(value) | zero, F32)


def _ffn_tail(x, y, mod_ref, ng2_ref, w_all, first_blk, fg_ref, side_work=None, h2_scr=None):
    x1 = x + mod_ref[:, 2 * D:3 * D] * y
    h2f = _rms_mod(x1, ng2_ref[...], mod_ref[:, 3 * D:4 * D], mod_ref[:, 4 * D:5 * D])
    h2 = h2f.astype(BF16)
    if side_work is not None:
        h2_scr[...] = h2
    acc = None
    token = None
    for c in range(FFN_CHUNKS):
        if side_work is not None:
            if token is not None:
                h2_scr[0:2 * SUBLANES, 0:LANES] = _pin_after(h2f[0:2 * SUBLANES, 0:LANES], token).astype(BF16)
                h2 = h2_scr[...]
            token = side_work(c, y if c == 0 else acc)
        hid = jnp.maximum(_dot(h2, w_all[first_blk + c]), 0.0)
        part = _dot((hid * hid).astype(BF16), w_all[first_blk + FFN_CHUNKS + c])
        acc = part if acc is None else acc + part
    x2 = x1 + mod_ref[:, 5 * D:6 * D] * acc
    if fg_ref is not None:
        x2 = x2 * lax.rsqrt(jnp.mean(x2 * x2, axis=-1, keepdims=True) + EPS) * fg_ref[...]
    return x2


IN_HBM = pl.BlockSpec(memory_space=pl.ANY)


def _ffn_blocks(w1_hbm, w2_hbm, layer):
    return ([w1_hbm.at[layer, :, pl.ds(c * WBLK, WBLK)] for c in range(FFN_CHUNKS)]
            + [w2_hbm.at[layer, pl.ds(c * WBLK, WBLK), :] for c in range(FFN_CHUNKS)])


def _stage_weights(srcs, w_all, stage, sem):
    copies = [pltpu.make_async_copy(src, stage.at[n % 2], sem.at[n % 2]) for n, src in enumerate(srcs)]
    copies[0].start()
    for n in range(len(srcs)):
        if n + 1 < len(srcs):
            copies[n + 1].start()
        copies[n].wait()
        w_all[n] = stage[n % 2].astype(BF16)


def _even_out_kernel(*refs, final, n_x, layer, j):
    x_refs, refs = refs[:n_x], refs[n_x:]
    a_ref, b_ref, mod_ref, ng2_ref, wo_hbm, w1_hbm, w2_hbm = refs[:7]
    fg_ref = refs[7] if final else None
    o_ref, w_all, stage, w_sem = refs[-4:]

    @pl.when(pl.program_id(0) == 0)
    def _():
        _stage_weights([wo_hbm.at[j]] + _ffn_blocks(w1_hbm, w2_hbm, layer), w_all, stage, w_sem)

    y = _dot(a_ref[...], w_all[0, 0:A_W, :]) + _dot(b_ref[...], w_all[0, A_W:A_W + B_W, :])
    o_ref[...] = _ffn_tail(_load_stream(x_refs, T_LAT // OUT_TILE), y, mod_ref, ng2_ref, w_all, 1, fg_ref)


def _even_out(a, b, xs, mod3, layer, ng2, w_out_all, j, w1_all, w2_all, n_out, final_g=None):
    x_specs, x_args = _stream_specs(xs, OUT_TILE)
    tok = lambda w: pl.BlockSpec((OUT_TILE, w), lambda i: (i, 0))
    in_specs = x_specs + [tok(A_W), tok(B_W), _mod_row_spec(layer, tiles_per_seq=SEQ // OUT_TILE),
                          _const_spec((1, D)), IN_HBM, IN_HBM, IN_HBM]
    args = x_args + [a, b, mod3, ng2, w_out_all, w1_all, w2_all]
    if final_g is not None:
        in_specs.append(_const_spec((1, D)))
        args.append(final_g)
    return pl.pallas_call(
        functools.partial(_even_out_kernel, final=final_g is not None, n_x=len(x_args), layer=layer, j=j),
        grid=(n_out // OUT_TILE,),
        in_specs=in_specs,
        out_specs=tok(D),
        out_shape=jax.ShapeDtypeStruct((n_out, D), F32),
        scratch_shapes=[pltpu.VMEM((1 + 2 * FFN_CHUNKS, WBLK, WBLK), BF16), pltpu.VMEM((2, WBLK, WBLK), F32),
                        pltpu.SemaphoreType.DMA((2,))],
        compiler_params=_params(("arbitrary",)),
    )(*args)


def _odd_kernel(*refs, final, n_tiles, layer, j):
    (xp_ref, xa_ref, xn_ref, xb_ref, moda_ref, modb_ref, ng1_ref, ng2_ref, wp1_hbm, bp1_ref, wdw_ref,
     bdw_ref, lg_ref, lb_ref, wp2_hbm, bp2_ref, w1_hbm, w2_hbm) = refs[:18]
    fg_ref = refs[18] if final else None
    o_ref, y_scr, conv_scr, w_scr, h2_scr, w_all, stage, w_sem = refs[-8:]
    i = pl.program_id(0)

    @pl.when(i == 0)
    def _():
        conv_scr[...] = jnp.zeros_like(conv_scr)
        _stage_weights([wp1_hbm.at[j, :, pl.ds(0, D)], wp1_hbm.at[j, :, pl.ds(D, D)], wp2_hbm.at[j]]
                       + _ffn_blocks(w1_hbm, w2_hbm, layer), w_all, stage, w_sem)

    ia = jnp.minimum(i, n_tiles - 1)
    is_lat = ia < LAT_TILES
    pos = ia % TILES_PER_SEQ
    seq_first = jnp.logical_or(jnp.logical_not(is_lat), pos == 0)
    seq_last = jnp.logical_or(jnp.logical_not(is_lat), pos == TILES_PER_SEQ - 1)
    xe = jnp.concatenate([xp_ref[...], xa_ref[...], xn_ref[...]], axis=0)
    h = _rms_mod(xe, ng1_ref[...], moda_ref[:, 0:D], moda_ref[:, D:2 * D]).astype(BF16)
    glu = (_dot(h, w_all[0]) + bp1_ref[:, 0:D]) * jax.nn.sigmoid(_dot(h, w_all[1]) + bp1_ref[:, D:2 * D])
    for cb in range(D // LANES):
        lanes = slice(cb * LANES, (cb + 1) * LANES)
        y_scr[cb, 0:HALO] = jnp.where(seq_first, 0.0, glu[0:HALO, lanes])
        y_scr[cb, HALO:HALO + TILE] = glu[HALO:HALO + TILE, lanes]
        y_scr[cb, HALO + TILE:] = jnp.where(seq_last, 0.0, glu[HALO + TILE:, lanes])

    acc = jnp.concatenate([conv_scr[cb] for cb in range(D // LANES)], axis=1)
    mu = jnp.mean(acc, axis=-1, keepdims=True)
    ac = acc - mu
    var = jnp.mean(ac * ac, axis=-1, keepdims=True)
    ln = ac * lax.rsqrt(var + EPS) * lg_ref[...] + lb_ref[...]
    act = (ln * jax.nn.sigmoid(ln)).astype(BF16)
    y = _dot(act, w_all[2]) + bp2_ref[...]

    base = HALO - CONV_K // 2
    rows_in = CONV_ROWS + SUBLANES

    row_blocks = TILE // CONV_ROWS

    def conv_piece(p, anchor):
        cb, r0 = p // row_blocks, (p % row_blocks) * CONV_ROWS
        lanes = slice(cb * LANES, (cb + 1) * LANES)
        w_scr[p, 0:CONV_K, :] = _pin_after(wdw_ref[:, lanes], anchor[0:1, 0:LANES])
        out = bdw_ref[:, lanes]
        for s in range(SUBLANES):
            part = None
            for off in range(s, base + CONV_K, SUBLANES):
                if off < base:
                    continue
                r1 = r0 + off - s
                term = y_scr[cb, r1:r1 + rows_in, :] * w_scr[p, off - base:off - base + 1, :]
                part = term if part is None else part + term
            out = out + part[s:s + CONV_ROWS]
        conv_scr[cb, r0:r0 + CONV_ROWS, :] = out
        return out[0:1, :]

    per_slot = CONV_PIECES // FFN_CHUNKS

    def conv_part(slot, anchor):
        for p in range(slot * per_slot, (slot + 1) * per_slot):
            token = conv_piece(p, anchor)
        return token

    o_ref[...] = _ffn_tail(xb_ref[...], y, modb_ref, ng2_ref, w_all, 3, fg_ref,
                           side_work=conv_part, h2_scr=h2_scr)


def _odd_layer(xs, mod3, layer, ng1, ng2, wp1_all, j, bp1, wdw, bdw, lg, lb, wp2_all, bp2, w1_all, w2_all,
               final_g=None):
    n = xs.shape[0]
    n_tiles = n // TILE
    per = TILE // HALO
    halo_blocks = n // HALO
    tile_a = lambda i: jnp.minimum(i, n_tiles - 1)
    tile_b = lambda i: jnp.maximum(i - 1, 0)
    in_specs = [
        pl.BlockSpec((HALO, D), lambda i: (jnp.maximum(tile_a(i) * per - 1, 0), 0)),
        pl.BlockSpec((TILE, D), lambda i: (tile_a(i), 0)),
        pl.BlockSpec((HALO, D), lambda i: (jnp.minimum((tile_a(i) + 1) * per, halo_blocks - 1), 0)),
        pl.BlockSpec((TILE, D), lambda i: (tile_b(i), 0)),
        _mod_row_spec(layer, tile_a), _mod_row_spec(layer, tile_b), _const_spec((1, D)), _const_spec((1, D)),
        IN_HBM, _const_spec((1, 2 * D)), _const_spec((CONV_K, D)), _const_spec((1, D)),
        _const_spec((1, D)), _const_spec((1, D)), IN_HBM, _const_spec((1, D)),
        IN_HBM, IN_HBM,
    ]
    args = [xs, xs, xs, xs, mod3, mod3, ng1, ng2, wp1_all, bp1, wdw, bdw, lg, lb, wp2_all, bp2, w1_all, w2_all]
    if final_g is not None:
        in_specs.append(_const_spec((1, D)))
        args.append(final_g)
    return pl.pallas_call(
        functools.partial(_odd_kernel, final=final_g is not None, n_tiles=n_tiles, layer=layer, j=j),
        grid=(n_tiles + 1,),
        in_specs=in_specs,
        out_specs=pl.BlockSpec((TILE, D), lambda i: (tile_b(i), 0)),
        out_shape=jax.ShapeDtypeStruct((n, D), F32),
        scratch_shapes=[pltpu.VMEM((D // LANES, HALO + TILE + HALO, LANES), F32),
                        pltpu.VMEM((D // LANES, TILE, LANES), F32),
                        pltpu.VMEM((CONV_PIECES, CONV_K + 1, LANES), F32), pltpu.VMEM((TILE, D), BF16),
                        pltpu.VMEM((3 + 2 * FFN_CHUNKS, WBLK, WBLK), BF16), pltpu.VMEM((2, WBLK, WBLK), F32),
                        pltpu.SemaphoreType.DMA((2,))],
        compiler_params=_params(("arbitrary",)),
    )(*args)


def kernel(x, c, ctx, c_ctx, w_mod, b_mod, norm_g, w_in, w_out, ln_v_g, w_sp, b_sp, rpb, w_pw1, b_pw1,
           w_dw, b_dw, ln_c_g, ln_c_b, w_pw2, b_pw2, w_ff1, w_ff2, final_g):
    row = lambda t: t.reshape(1, -1)
    cc = jnp.concatenate([c, c_ctx[None, :], jnp.zeros((MOD_ROWS - BATCH - 1, D), F32)], axis=0)
    mod3 = _modulation(cc, w_mod, b_mod).reshape(DEPTH * MOD_ROWS, 1, 6 * D)
    xs = (x.reshape(T_LAT, D), ctx.reshape(T_CTX, D))
    w_in, w_sp = w_in.astype(BF16), w_sp.astype(BF16)
    last_reader = ((DEPTH - 1) // 2) * 2
    for l in range(DEPTH):
        ctx_out = l < last_reader
        fg = row(final_g) if l == DEPTH - 1 else None
        j = l // 2
        if l % 2 == 0:
            a, q, k, v = _even_in(xs, mod3, l, row(norm_g[l, 0]), w_in, j, row(ln_v_g[j]),
                                  w_sp, b_sp[j][:, :, None])
            b = _attention(q, k, v, _bias_tables(rpb[j]), ctx_out)
            xs = _even_out(a, b, xs, mod3, l, row(norm_g[l, 1]), w_out, j, w_ff1, w_ff2,
                           T_ALL if ctx_out else T_LAT, fg)
        else:
            xs = _odd_layer(xs, mod3, l, row(norm_g[l, 0]), row(norm_g[l, 1]), w_pw1, j,
                            row(b_pw1[j]), w_dw[j], row(b_dw[j]), row(ln_c_g[j]), row(ln_c_b[j]),
                            w_pw2, row(b_pw2[j]), w_ff1, w_ff2, fg)
    return xs.reshape(BATCH, SEQ, D)
```

```python
import functools

import jax
import jax.numpy as jnp
from jax import lax
from jax.experimental import pallas as pl
from jax.experimental.pallas import tpu as pltpu

D = 1024
BATCH = 4
SEQ = 4096
DEPTH = 4
GRID_W = 64
ROWS = SEQ // GRID_W
CTX_LEN = 256
CHUNK = 128
A_GROUPS = 4
A_W = 512
B_W = 512
HEAD_DIM = 64
HEADS = 8
WIN_H = 8
WIN_W = 16
Q_ROWS = 4
Q_BLK = Q_ROWS * GRID_W
K_ROWS = Q_ROWS + WIN_H
K_BLK = K_ROWS * GRID_W
ROW_STEPS = ROWS // Q_ROWS
MIX_IN = 2 * A_W + 3 * B_W
CONV_K = 31
HALO = 16
SUBLANES = 8
LANES = 128
CONV_ROWS = 128
PW1_COLS = 256
FFN = 4 * D
FFN_CHUNK = 1024
FFN_CHUNKS = FFN // FFN_CHUNK
WBLK = 1024
assert WBLK == D == FFN_CHUNK == A_W + B_W
EPS = 1e-6
NEG = -1e30
LOG2E = 1.4426950408889634

T_LAT = BATCH * SEQ
T_CTX = BATCH * CTX_LEN
T_ALL = T_LAT + T_CTX
TILE = 256
IN_TILE = 1024
OUT_TILE = 512
CONV_PIECES = (D // LANES) * (TILE // CONV_ROWS)
TILES_PER_SEQ = SEQ // TILE
LAT_TILES = T_LAT // TILE
MOD_ROWS = 8
MOD_N = 1536
VMEM_LIMIT = 56 * 1024 * 1024

BF16 = jnp.bfloat16
F32 = jnp.float32


def _const_spec(shape):
    zeros = (0,) * len(shape)
    return pl.BlockSpec(shape, lambda *_: zeros, pipeline_mode=pl.Buffered(1))


def _layer_spec(stacked, idx):
    shape = stacked.shape[1:]
    zeros = (0,) * len(shape)
    return pl.BlockSpec((None,) + shape, lambda *_: (idx,) + zeros, pipeline_mode=pl.Buffered(1))


def _params(sem):
    return pltpu.CompilerParams(dimension_semantics=sem, vmem_limit_bytes=VMEM_LIMIT)


def _mod_row_spec(layer, tile_of=lambda i: i, tiles_per_seq=TILES_PER_SEQ):
    return pl.BlockSpec(
        (None, 1, 6 * D),
        lambda i: (layer * MOD_ROWS + jnp.minimum(tile_of(i) // tiles_per_seq, BATCH), 0, 0))


def _rms_mod(x, g, shift, scale):
    return x * lax.rsqrt(jnp.mean(x * x, axis=-1, keepdims=True) + EPS) * (g * (1.0 + scale)) + shift


def _dot(a, b):
    return jnp.dot(a, b, preferred_element_type=F32)


def _dot_nt(a, b):
    return lax.dot_general(a, b, (((1,), (1,)), ((), ())), preferred_element_type=F32)


def _mod_kernel(cc_ref, w_ref, b_ref, o_ref):
    cc = cc_ref[...]
    sc = (cc * jax.nn.sigmoid(cc)).astype(BF16)
    o_ref[...] = _dot(sc, w_ref[...].astype(BF16)) + b_ref[...]


def _modulation(cc, w_mod, b_mod):
    nblk = (6 * D) // MOD_N
    return pl.pallas_call(
        _mod_kernel,
        grid=(DEPTH, nblk),
        in_specs=[
            pl.BlockSpec((MOD_ROWS, D), lambda l, n: (0, 0)),
            pl.BlockSpec((None, D, MOD_N), lambda l, n: (l, 0, n)),
            pl.BlockSpec((None, 1, MOD_N), lambda l, n: (l, 0, n)),
        ],
        out_specs=pl.BlockSpec((None, MOD_ROWS, MOD_N), lambda l, n: (l, 0, n)),
        out_shape=jax.ShapeDtypeStruct((DEPTH, MOD_ROWS, 6 * D), F32),
        compiler_params=_params(("arbitrary", "arbitrary")),
    )(cc, w_mod, b_mod.reshape(DEPTH, 1, 6 * D))


def _load_stream(x_refs, lat_tiles):
    if len(x_refs) == 1:
        return x_refs[0][...]
    return jnp.where(pl.program_id(0) < lat_tiles, x_refs[0][...], x_refs[1][...])


def _stream_specs(xs, tile):
    if not isinstance(xs, tuple):
        return [pl.BlockSpec((tile, D), lambda i: (i, 0))], [xs]
    lat_tiles = xs[0].shape[0] // tile
    return [pl.BlockSpec((tile, D), lambda i: (jnp.minimum(i, lat_tiles - 1), 0)),
            pl.BlockSpec((tile, D), lambda i: (jnp.maximum(i - lat_tiles, 0), 0))], list(xs)


def _even_in_kernel(*refs, n_x):
    x_refs = refs[:n_x]
    mod_ref, ng_ref, w_in_ref, lng_ref, w_sp_ref, b_sp_ref, a_ref, q_ref, k_ref, v_ref = refs[n_x:]
    x = _load_stream(x_refs, T_LAT // IN_TILE)
    h = _rms_mod(x, ng_ref[...], mod_ref[:, 0:D], mod_ref[:, D:2 * D]).astype(BF16)
    zp = _dot(h, w_in_ref[:, 0:2 * A_W])
    c0 = 2 * A_W
    q_ref[...] = (_dot(h, w_in_ref[:, c0:c0 + B_W]) * (HEAD_DIM ** -0.5 * LOG2E)).astype(BF16)
    k_ref[...] = _dot(h, w_in_ref[:, c0 + B_W:c0 + 2 * B_W]).astype(BF16)
    v_ref[...] = _dot(h, w_in_ref[:, c0 + 2 * B_W:c0 + 3 * B_W]).astype(BF16)
    z = jax.nn.gelu(zp)
    gd = A_W // A_GROUPS
    for grp in range(A_GROUPS):
        g = z[:, A_W + grp * gd:A_W + (grp + 1) * gd]
        mu = jnp.mean(g, axis=-1, keepdims=True)
        gc = g - mu
        var = jnp.mean(gc * gc, axis=-1, keepdims=True)
        gn = (gc * lax.rsqrt(var + EPS) * lng_ref[:, grp * gd:(grp + 1) * gd]).astype(BF16)
        for ch in range(IN_TILE // CHUNK):
            rows = slice(ch * CHUNK, (ch + 1) * CHUNK)
            s = _dot(w_sp_ref[grp], gn[rows]) + b_sp_ref[grp]
            a_ref[rows, grp * gd:(grp + 1) * gd] = (z[rows, grp * gd:(grp + 1) * gd] * s).astype(BF16)


def _even_in(xs, mod3, layer, ng, w_in_all, j, lng, w_sp_all, b_sp):
    x_specs, x_args = _stream_specs(xs, IN_TILE)
    tok = lambda w: pl.BlockSpec((IN_TILE, w), lambda i: (i, 0))
    out = jax.ShapeDtypeStruct((T_ALL, A_W), BF16)
    return pl.pallas_call(
        functools.partial(_even_in_kernel, n_x=len(x_args)),
        grid=(T_ALL // IN_TILE,),
        in_specs=x_specs + [
            _mod_row_spec(layer, tiles_per_seq=SEQ // IN_TILE), _const_spec((1, D)), _layer_spec(w_in_all, j),
            _const_spec((1, A_W)), _layer_spec(w_sp_all, j), _const_spec((A_GROUPS, CHUNK, 1)),
        ],
        out_specs=[tok(A_W)] * 4,
        out_shape=[out] * 4,
        compiler_params=_params(("arbitrary",)),
    )(*x_args, mod3, ng, w_in_all, lng, w_sp_all, b_sp)


def _attn_kernel(q_ref, k_ref, v_ref, kc_ref, vc_ref, tbl_ref, o_ref, *, ctx_queries):
    r = pl.program_id(1)
    lane = lax.broadcasted_iota(jnp.int32, (Q_BLK, 2 * HEAD_DIM), 1)
    first_head = lane < HEAD_DIM

    def attend(local):
        if local:
            start = pl.multiple_of(
                jnp.clip(r * Q_ROWS - WIN_H // 2, 0, ROWS - K_ROWS) * GRID_W, GRID_W)

        def pair_scores(hp):
            cols = slice(hp * 2 * HEAD_DIM, (hp + 1) * 2 * HEAD_DIM)
            q2 = q_ref[:, cols]
            k2 = kc_ref[:, cols]
            if local:
                k2 = jnp.concatenate([k_ref[pl.ds(start, K_BLK), cols], k2], axis=0)
            zero = jnp.zeros_like(q2)
            return _dot_nt(jnp.concatenate([jnp.where(first_head, q2, zero), jnp.where(first_head, zero, q2)],
                                           axis=0), k2)

        def pair_output(hp, s2):
            cols = slice(hp * 2 * HEAD_DIM, (hp + 1) * 2 * HEAD_DIM)
            v2 = vc_ref[:, cols]
            if local:
                v2 = jnp.concatenate([v_ref[pl.ds(start, K_BLK), cols], v2], axis=0)
            ps, dens = [], []
            for half in range(2):
                s = s2[half * Q_BLK:(half + 1) * Q_BLK]
                if local:
                    s = jnp.concatenate([s[:, 0:K_BLK] + tbl_ref[2 * hp + half], s[:, K_BLK:]], axis=1)
                m = jnp.max(s, axis=-1, keepdims=True)
                p = jnp.exp2(s - m)
                dens.append(jnp.sum(p, axis=-1, keepdims=True))
                ps.append(p.astype(BF16))
            o2 = _dot(jnp.concatenate(ps, axis=0), v2)
            o_ref[:, cols] = jnp.where(first_head, o2[0:Q_BLK] / dens[0], o2[Q_BLK:] / dens[1]).astype(BF16)

        s2 = pair_scores(0)
        for hp in range(HEADS // 2):
            s2_next = pair_scores(hp + 1) if hp + 1 < HEADS // 2 else None
            pair_output(hp, s2)
            s2 = s2_next

    if ctx_queries:
        pl.when(r < ROW_STEPS)(lambda: attend(True))
        pl.when(r >= ROW_STEPS)(lambda: attend(False))
    else:
        attend(True)


def _attention(q, k, v, tbl, ctx_queries):
    steps = ROW_STEPS + (1 if ctx_queries else 0)
    lat_blocks = T_LAT // Q_BLK

    def q_idx(b, r):
        return (jnp.where(r < ROW_STEPS, b * ROW_STEPS + r, lat_blocks + b), 0)

    def tbl_idx(b, r):
        return (jnp.where(r == 0, 0, jnp.where(r >= ROW_STEPS - 1, 2, 1)), 0, 0, 0)

    ctx_blk = T_LAT // CTX_LEN
    return pl.pallas_call(
        functools.partial(_attn_kernel, ctx_queries=ctx_queries),
        grid=(BATCH, steps),
        in_specs=[
            pl.BlockSpec((Q_BLK, B_W), q_idx),
            pl.BlockSpec((SEQ, B_W), lambda b, r: (b, 0)),
            pl.BlockSpec((SEQ, B_W), lambda b, r: (b, 0)),
            pl.BlockSpec((CTX_LEN, B_W), lambda b, r: (ctx_blk + b, 0)),
            pl.BlockSpec((CTX_LEN, B_W), lambda b, r: (ctx_blk + b, 0)),
            pl.BlockSpec((None, HEADS, Q_BLK, K_BLK), tbl_idx),
        ],
        out_specs=pl.BlockSpec((Q_BLK, B_W), q_idx),
        out_shape=jax.ShapeDtypeStruct((T_ALL if ctx_queries else T_LAT, B_W), BF16),
        compiler_params=_params(("arbitrary", "arbitrary")),
    )(q, k, v, k, v, tbl)


N_DR = 2 * WIN_H - 1
N_DC = 2 * WIN_W - 1
TBL_R0 = (0, Q_ROWS, ROWS - Q_ROWS)


def _tbl_kernel(rpb_ref, o_ref, t_scr):
    h = pl.program_id(0)
    shape = (GRID_W, 2 * GRID_W)
    cq = lax.broadcasted_iota(jnp.int32, shape, 0)
    lane = lax.broadcasted_iota(jnp.int32, shape, 1)
    ck = lane & (GRID_W - 1)
    dc = jnp.clip(ck - cq + WIN_W - 1, 0, N_DC - 1)
    cs = jnp.clip(cq - WIN_W // 2, 0, GRID_W - WIN_W)
    col_ok = (ck >= cs) & (ck < cs + WIN_W)
    neg = jnp.full(shape, NEG, F32)
    for dr in range(N_DR):
        t = neg
        for j in range(N_DC):
            t = jnp.where(dc == j, rpb_ref[(h * N_DR + dr) * N_DC + j], t)
        t_scr[dr] = jnp.where(col_ok, t * LOG2E, NEG)
    left = lane < GRID_W
    for v, r0 in enumerate(TBL_R0):
        k0 = min(max(r0 - WIN_H // 2, 0), ROWS - K_ROWS)
        for qi in range(Q_ROWS):
            qrow = r0 + qi
            w0 = min(max(qrow - WIN_H // 2, 0), ROWS - WIN_H)
            for kp in range(K_ROWS // 2):
                halves = []
                for krow in (k0 + 2 * kp, k0 + 2 * kp + 1):
                    in_win = w0 <= krow < w0 + WIN_H
                    halves.append(t_scr[krow - qrow + WIN_H - 1] if in_win else neg)
                o_ref[v, qi * GRID_W:(qi + 1) * GRID_W, kp * 2 * GRID_W:(kp + 1) * 2 * GRID_W] = (
                    jnp.where(left, halves[0], halves[1]))


def _bias_tables(rpb):
    return pl.pallas_call(
        _tbl_kernel,
        grid=(HEADS,),
        in_specs=[pl.BlockSpec(memory_space=pltpu.SMEM)],
        out_specs=pl.BlockSpec((len(TBL_R0), None, Q_BLK, K_BLK), lambda h: (0, h, 0, 0)),
        out_shape=jax.ShapeDtypeStruct((len(TBL_R0), HEADS, Q_BLK, K_BLK), F32),
        scratch_shapes=[pltpu.VMEM((N_DR, GRID_W, 2 * GRID_W), F32)],
        compiler_params=_params(("arbitrary",)),
    )(rpb.reshape(-1))


def _pin_after(value, anchor):
    bits = lambda t: pltpu.bitcast(t, jnp.uint32)
    zero = lax.shift_right_logical(lax.shift_right_logical(bits(anchor), jnp.uint32(16)), jnp.uint32(16))
    return pltpu.bitcast(bits(value) | zero, F32)


def _ffn_tail(x, y, mod_ref, ng2_ref, w_all, first_blk, fg_ref, side_work=None, h2_scr=None):
    x1 = x + mod_ref[:, 2 * D:3 * D] * y
    h2f = _rms_mod(x1, ng2_ref[...], mod_ref[:, 3 * D:4 * D], mod_ref[:, 4 * D:5 * D])
    h2 = h2f.astype(BF16)
    if side_work is not None:
        h2_scr[...] = h2
    acc = None
    token = None
    for c in range(FFN_CHUNKS):
        if side_work is not None:
            if token is not None:
                h2_scr[0:2 * SUBLANES, 0:LANES] = _pin_after(h2f[0:2 * SUBLANES, 0:LANES], token).astype(BF16)
                h2 = h2_scr[...]
            token = side_work(c, y if c == 0 else acc)
        hid = jnp.maximum(_dot(h2, w_all[first_blk + c]), 0.0)
        part = _dot((hid * hid).astype(BF16), w_all[first_blk + FFN_CHUNKS + c])
        acc = part if acc is None else acc + part
    x2 = x1 + mod_ref[:, 5 * D:6 * D] * acc
    if fg_ref is not None:
        x2 = x2 * lax.rsqrt(jnp.mean(x2 * x2, axis=-1, keepdims=True) + EPS) * fg_ref[...]
    return x2


IN_HBM = pl.BlockSpec(memory_space=pl.ANY)


def _ffn_blocks(w1_hbm, w2_hbm, layer):
    return ([w1_hbm.at[layer, :, pl.ds(c * WBLK, WBLK)] for c in range(FFN_CHUNKS)]
            + [w2_hbm.at[layer, pl.ds(c * WBLK, WBLK), :] for c in range(FFN_CHUNKS)])


def _stage_weights(srcs, w_all, stage, sem):
    copies = [pltpu.make_async_copy(src, stage.at[n % 2], sem.at[n % 2]) for n, src in enumerate(srcs)]
    copies[0].start()
    for n in range(len(srcs)):
        if n + 1 < len(srcs):
            copies[n + 1].start()
        copies[n].wait()
        w_all[n] = stage[n % 2].astype(BF16)


def _even_out_kernel(*refs, final, n_x, layer, j):
    x_refs, refs = refs[:n_x], refs[n_x:]
    a_ref, b_ref, mod_ref, ng2_ref, wo_hbm, w1_hbm, w2_hbm = refs[:7]
    fg_ref = refs[7] if final else None
    o_ref, w_all, stage, w_sem = refs[-4:]

    @pl.when(pl.program_id(0) == 0)
    def _():
        _stage_weights([wo_hbm.at[j]] + _ffn_blocks(w1_hbm, w2_hbm, layer), w_all, stage, w_sem)

    y = _dot(a_ref[...], w_all[0, 0:A_W, :]) + _dot(b_ref[...], w_all[0, A_W:A_W + B_W, :])
    o_ref[...] = _ffn_tail(_load_stream(x_refs, T_LAT // OUT_TILE), y, mod_ref, ng2_ref, w_all, 1, fg_ref)


def _even_out(a, b, xs, mod3, layer, ng2, w_out_all, j, w1_all, w2_all, n_out, final_g=None):
    x_specs, x_args = _stream_specs(xs, OUT_TILE)
    tok = lambda w: pl.BlockSpec((OUT_TILE, w), lambda i: (i, 0))
    in_specs = x_specs + [tok(A_W), tok(B_W), _mod_row_spec(layer, tiles_per_seq=SEQ // OUT_TILE),
                          _const_spec((1, D)), IN_HBM, IN_HBM, IN_HBM]
    args = x_args + [a, b, mod3, ng2, w_out_all, w1_all, w2_all]
    if final_g is not None:
        in_specs.append(_const_spec((1, D)))
        args.append(final_g)
    return pl.pallas_call(
        functools.partial(_even_out_kernel, final=final_g is not None, n_x=len(x_args), layer=layer, j=j),
        grid=(n_out // OUT_TILE,),
        in_specs=in_specs,
        out_specs=tok(D),
        out_shape=jax.ShapeDtypeStruct((n_out, D), F32),
        scratch_shapes=[pltpu.VMEM((1 + 2 * FFN_CHUNKS, WBLK, WBLK), BF16), pltpu.VMEM((2, WBLK, WBLK), F32),
                        pltpu.SemaphoreType.DMA((2,))],
        compiler_params=_params(("arbitrary",)),
    )(*args)


def _odd_kernel(*refs, final, n_tiles, layer, j):
    (xp_ref, xa_ref, xn_ref, xb_ref, moda_ref, modb_ref, ng1_ref, ng2_ref, wp1_hbm, bp1_ref, wdw_ref,
     bdw_ref, lg_ref, lb_ref, wp2_hbm, bp2_ref, w1_hbm, w2_hbm) = refs[:18]
    fg_ref = refs[18] if final else None
    o_ref, y_scr, conv_scr, w_scr, h2_scr, w_all, stage, w_sem = refs[-8:]
    i = pl.program_id(0)

    @pl.when(i == 0)
    def _():
        conv_scr[...] = jnp.zeros_like(conv_scr)
        _stage_weights([wp1_hbm.at[j, :, pl.ds(0, D)], wp1_hbm.at[j, :, pl.ds(D, D)], wp2_hbm.at[j]]
                       + _ffn_blocks(w1_hbm, w2_hbm, layer), w_all, stage, w_sem)

    ia = jnp.minimum(i, n_tiles - 1)
    is_lat = ia < LAT_TILES
    pos = ia % TILES_PER_SEQ
    seq_first = jnp.logical_or(jnp.logical_not(is_lat), pos == 0)
    seq_last = jnp.logical_or(jnp.logical_not(is_lat), pos == TILES_PER_SEQ - 1)
    xe = jnp.concatenate([xp_ref[...], xa_ref[...], xn_ref[...]], axis=0)
    h = _rms_mod(xe, ng1_ref[...], moda_ref[:, 0:D], moda_ref[:, D:2 * D]).astype(BF16)
    for c0 in range(0, D, PW1_COLS):
        cols = slice(c0, c0 + PW1_COLS)
        gate = _dot(h, w_all[1, :, cols]) + bp1_ref[:, D + c0:D + c0 + PW1_COLS]
        glu = (_dot(h, w_all[0, :, cols]) + bp1_ref[:, cols]) * jax.nn.sigmoid(gate)
        for k in range(PW1_COLS // LANES):
            cb, lanes = c0 // LANES + k, slice(k * LANES, (k + 1) * LANES)
            y_scr[cb, 0:HALO] = jnp.where(seq_first, 0.0, glu[0:HALO, lanes])
            y_scr[cb, HALO:HALO + TILE] = glu[HALO:HALO + TILE, lanes]
            y_scr[cb, HALO + TILE:] = jnp.where(seq_last, 0.0, glu[HALO + TILE:, lanes])

    conv_rows = lambda: jnp.concatenate([conv_scr[cb] for cb in range(D // LANES)], axis=1)
    mu = jnp.mean(conv_rows(), axis=-1, keepdims=True)
    ac = conv_rows() - mu
    var = jnp.mean(ac * ac, axis=-1, keepdims=True)
    ac = conv_rows() - mu
    ln = ac * lax.rsqrt(var + EPS) * lg_ref[...] + lb_ref[...]
    act = (ln * jax.nn.sigmoid(ln)).astype(BF16)
    y = _dot(act, w_all[2]) + bp2_ref[...]

    base = HALO - CONV_K // 2
    rows_in = CONV_ROWS + SUBLANES

    row_blocks = TILE // CONV_ROWS

    def conv_piece(p, anchor):
        cb, r0 = p // row_blocks, (p % row_blocks) * CONV_ROWS
        lanes = slice(cb * LANES, (cb + 1) * LANES)
        w_scr[p, 0:CONV_K, :] = _pin_after(wdw_ref[:, lanes], anchor[0:1, 0:LANES])
        out = bdw_ref[:, lanes]
        for s in range(SUBLANES):
            part = None
            for off in range(s, base + CONV_K, SUBLANES):
                if off < base:
                    continue
                r1 = r0 + off - s
                term = y_scr[cb, r1:r1 + rows_in, :] * w_scr[p, off - base:off - base + 1, :]
                part = term if part is None else part + term
            out = out + part[s:s + CONV_ROWS]
        conv_scr[cb, r0:r0 + CONV_ROWS, :] = out
        return out[0:1, :]

    per_slot = CONV_PIECES // FFN_CHUNKS

    def conv_part(slot, anchor):
        for p in range(slot * per_slot, (slot + 1) * per_slot):
            token = conv_piece(p, anchor)
        return token

    o_ref[...] = _ffn_tail(xb_ref[...], y, modb_ref, ng2_ref, w_all, 3, fg_ref,
                           side_work=conv_part, h2_scr=h2_scr)


def _odd_layer(xs, mod3, layer, ng1, ng2, wp1_all, j, bp1, wdw, bdw, lg, lb, wp2_all, bp2, w1_all, w2_all,
               final_g=None):
    n = xs.shape[0]
    n_tiles = n // TILE
    per = TILE // HALO
    halo_blocks = n // HALO
    tile_a = lambda i: jnp.minimum(i, n_tiles - 1)
    tile_b = lambda i: jnp.maximum(i - 1, 0)
    in_specs = [
        pl.BlockSpec((HALO, D), lambda i: (jnp.maximum(tile_a(i) * per - 1, 0), 0)),
        pl.BlockSpec((TILE, D), lambda i: (tile_a(i), 0)),
        pl.BlockSpec((HALO, D), lambda i: (jnp.minimum((tile_a(i) + 1) * per, halo_blocks - 1), 0)),
        pl.BlockSpec((TILE, D), lambda i: (tile_b(i), 0)),
        _mod_row_spec(layer, tile_a), _mod_row_spec(layer, tile_b), _const_spec((1, D)), _const_spec((1, D)),
        IN_HBM, _const_spec((1, 2 * D)), _const_spec((CONV_K, D)), _const_spec((1, D)),
        _const_spec((1, D)), _const_spec((1, D)), IN_HBM, _const_spec((1, D)),
        IN_HBM, IN_HBM,
    ]
    args = [xs, xs, xs, xs, mod3, mod3, ng1, ng2, wp1_all, bp1, wdw, bdw, lg, lb, wp2_all, bp2, w1_all, w2_all]
    if final_g is not None:
        in_specs.append(_const_spec((1, D)))
        args.append(final_g)
    return pl.pallas_call(
        functools.partial(_odd_kernel, final=final_g is not None, n_tiles=n_tiles, layer=layer, j=j),
        grid=(n_tiles + 1,),
        in_specs=in_specs,
        out_specs=pl.BlockSpec((TILE, D), lambda i: (tile_b(i), 0)),
        out_shape=jax.ShapeDtypeStruct((n, D), F32),
        scratch_shapes=[pltpu.VMEM((D // LANES, HALO + TILE + HALO, LANES), F32),
                        pltpu.VMEM((D // LANES, TILE, LANES), F32),
                        pltpu.VMEM((CONV_PIECES, CONV_K + 1, LANES), F32), pltpu.VMEM((TILE, D), BF16),
                        pltpu.VMEM((3 + 2 * FFN_CHUNKS, WBLK, WBLK), BF16), pltpu.VMEM((2, WBLK, WBLK), F32),
                        pltpu.SemaphoreType.DMA((2,))],
        compiler_params=_params(("arbitrary",)),
    )(*args)


def kernel(x, c, ctx, c_ctx, w_mod, b_mod, norm_g, w_in, w_out, ln_v_g, w_sp, b_sp, rpb, w_pw1, b_pw1,
           w_dw, b_dw, ln_c_g, ln_c_b, w_pw2, b_pw2, w_ff1, w_ff2, final_g):
    row = lambda t: t.reshape(1, -1)
    cc = jnp.concatenate([c, c_ctx[None, :], jnp.zeros((MOD_ROWS - BATCH - 1, D), F32)], axis=0)
    mod3 = _modulation(cc, w_mod, b_mod).reshape(DEPTH * MOD_ROWS, 1, 6 * D)
    xs = (x.reshape(T_LAT, D), ctx.reshape(T_CTX, D))
    w_in, w_sp = w_in.astype(BF16), w_sp.astype(BF16)
    last_reader = ((DEPTH - 1) // 2) * 2
    for l in range(DEPTH):
        ctx_out = l < last_reader
        fg = row(final_g) if l == DEPTH - 1 else None
        j = l // 2
        if l % 2 == 0:
            a, q, k, v = _even_in(xs, mod3, l, row(norm_g[l, 0]), w_in, j, row(ln_v_g[j]),
                                  w_sp, b_sp[j][:, :, None])
            b = _attention(q, k, v, _bias_tables(rpb[j]), ctx_out)
            xs = _even_out(a, b, xs, mod3, l, row(norm_g[l, 1]), w_out, j, w_ff1, w_ff2,
                           T_ALL if ctx_out else T_LAT, fg)
        else:
            xs = _odd_layer(xs, mod3, l, row(norm_g[l, 0]), row(norm_g[l, 1]), w_pw1, j,
                            row(b_pw1[j]), w_dw[j], row(b_dw[j]), row(ln_c_g[j]), row(ln_c_b[j]),
                            w_pw2, row(b_pw2[j]), w_ff1, w_ff2, fg)
    return xs.reshape(BATCH, SEQ, D)
```

```python
import functools

import jax
import jax.numpy as jnp
from jax import lax
from jax.experimental import pallas as pl
from jax.experimental.pallas import tpu as pltpu

D = 1024
BATCH = 4
SEQ = 4096
DEPTH = 4
GRID_W = 64
ROWS = SEQ // GRID_W
CTX_LEN = 256
CHUNK = 128
A_GROUPS = 4
A_W = 512
B_W = 512
HEAD_DIM = 64
HEADS = 8
WIN_H = 8
WIN_W = 16
Q_ROWS = 4
Q_BLK = Q_ROWS * GRID_W
K_ROWS = Q_ROWS + WIN_H
K_BLK = K_ROWS * GRID_W
ROW_STEPS = ROWS // Q_ROWS
MIX_IN = 2 * A_W + 3 * B_W
CONV_K = 31
HALO = 16
SUBLANES = 8
LANES = 128
CONV_ROWS = 128
PW1_COLS = 256
GATE_COLS = 256
FFN = 4 * D
FFN_CHUNK = 1024
FFN_CHUNKS = FFN // FFN_CHUNK
WBLK = 1024
assert WBLK == D == FFN_CHUNK == A_W + B_W
EPS = 1e-6
NEG = -1e30
LOG2E = 1.4426950408889634

T_LAT = BATCH * SEQ
T_CTX = BATCH * CTX_LEN
T_ALL = T_LAT + T_CTX
TILE = 256
IN_TILE = 1024
OUT_TILE = 512
CONV_PIECES = (D // LANES) * (TILE // CONV_ROWS)
TILES_PER_SEQ = SEQ // TILE
LAT_TILES = T_LAT // TILE
MOD_ROWS = 8
MOD_N = 1536
VMEM_LIMIT = 56 * 1024 * 1024

BF16 = jnp.bfloat16
F32 = jnp.float32


def _const_spec(shape):
    zeros = (0,) * len(shape)
    return pl.BlockSpec(shape, lambda *_: zeros, pipeline_mode=pl.Buffered(1))


def _layer_spec(stacked, idx):
    shape = stacked.shape[1:]
    zeros = (0,) * len(shape)
    return pl.BlockSpec((None,) + shape, lambda *_: (idx,) + zeros, pipeline_mode=pl.Buffered(1))


def _params(sem):
    return pltpu.CompilerParams(dimension_semantics=sem, vmem_limit_bytes=VMEM_LIMIT)


def _mod_row_spec(layer, tile_of=lambda i: i, tiles_per_seq=TILES_PER_SEQ):
    return pl.BlockSpec(
        (None, 1, 6 * D),
        lambda i: (layer * MOD_ROWS + jnp.minimum(tile_of(i) // tiles_per_seq, BATCH), 0, 0))


def _rms_mod(x, g, shift, scale):
    return x * lax.rsqrt(jnp.mean(x * x, axis=-1, keepdims=True) + EPS) * (g * (1.0 + scale)) + shift


def _dot(a, b):
    return jnp.dot(a, b, preferred_element_type=F32)


def _dot_nt(a, b):
    return lax.dot_general(a, b, (((1,), (1,)), ((), ())), preferred_element_type=F32)


def _mod_kernel(cc_ref, w_ref, b_ref, o_ref):
    cc = cc_ref[...]
    sc = (cc * jax.nn.sigmoid(cc)).astype(BF16)
    o_ref[...] = _dot(sc, w_ref[...].astype(BF16)) + b_ref[...]


def _modulation(cc, w_mod, b_mod):
    nblk = (6 * D) // MOD_N
    return pl.pallas_call(
        _mod_kernel,
        grid=(DEPTH, nblk),
        in_specs=[
            pl.BlockSpec((MOD_ROWS, D), lambda l, n: (0, 0)),
            pl.BlockSpec((None, D, MOD_N), lambda l, n: (l, 0, n)),
            pl.BlockSpec((None, 1, MOD_N), lambda l, n: (l, 0, n)),
        ],
        out_specs=pl.BlockSpec((None, MOD_ROWS, MOD_N), lambda l, n: (l, 0, n)),
        out_shape=jax.ShapeDtypeStruct((DEPTH, MOD_ROWS, 6 * D), F32),
        compiler_params=_params(("arbitrary", "arbitrary")),
    )(cc, w_mod, b_mod.reshape(DEPTH, 1, 6 * D))


def _load_stream(x_refs, lat_tiles):
    if len(x_refs) == 1:
        return x_refs[0][...]
    return jnp.where(pl.program_id(0) < lat_tiles, x_refs[0][...], x_refs[1][...])


def _stream_specs(xs, tile):
    if not isinstance(xs, tuple):
        return [pl.BlockSpec((tile, D), lambda i: (i, 0))], [xs]
    lat_tiles = xs[0].shape[0] // tile
    return [pl.BlockSpec((tile, D), lambda i: (jnp.minimum(i, lat_tiles - 1), 0)),
            pl.BlockSpec((tile, D), lambda i: (jnp.maximum(i - lat_tiles, 0), 0))], list(xs)


def _even_in_kernel(*refs, n_x):
    x_refs = refs[:n_x]
    mod_ref, ng_ref, w_in_ref, lng_ref, w_sp_ref, b_sp_ref, a_ref, q_ref, k_ref, v_ref = refs[n_x:]
    x = _load_stream(x_refs, T_LAT // IN_TILE)
    h = _rms_mod(x, ng_ref[...], mod_ref[:, 0:D], mod_ref[:, D:2 * D]).astype(BF16)
    zps = [(_dot(h, w_in_ref[:, c:c + GATE_COLS]), _dot(h, w_in_ref[:, A_W + c:A_W + c + GATE_COLS]))
           for c in range(0, A_W, GATE_COLS)]
    c0 = 2 * A_W
    q_ref[...] = (_dot(h, w_in_ref[:, c0:c0 + B_W]) * (HEAD_DIM ** -0.5 * LOG2E)).astype(BF16)
    k_ref[...] = _dot(h, w_in_ref[:, c0 + B_W:c0 + 2 * B_W]).astype(BF16)
    v_ref[...] = _dot(h, w_in_ref[:, c0 + 2 * B_W:c0 + 3 * B_W]).astype(BF16)
    gd = A_W // A_GROUPS
    per_chunk = GATE_COLS // gd
    for grp in range(A_GROUPS):
        zu, zg = zps[grp // per_chunk]
        k0 = (grp % per_chunk) * gd
        u = jax.nn.gelu(zu[:, k0:k0 + gd])
        g = jax.nn.gelu(zg[:, k0:k0 + gd])
        mu = jnp.mean(g, axis=-1, keepdims=True)
        gc = g - mu
        var = jnp.mean(gc * gc, axis=-1, keepdims=True)
        gn = (gc * lax.rsqrt(var + EPS) * lng_ref[:, grp * gd:(grp + 1) * gd]).astype(BF16)
        for ch in range(IN_TILE // CHUNK):
            rows = slice(ch * CHUNK, (ch + 1) * CHUNK)
            s = _dot(w_sp_ref[grp], gn[rows]) + b_sp_ref[grp]
            a_ref[rows, grp * gd:(grp + 1) * gd] = (u[rows] * s).astype(BF16)


def _even_in(xs, mod3, layer, ng, w_in_all, j, lng, w_sp_all, b_sp):
    x_specs, x_args = _stream_specs(xs, IN_TILE)
    tok = lambda w: pl.BlockSpec((IN_TILE, w), lambda i: (i, 0))
    out = jax.ShapeDtypeStruct((T_ALL, A_W), BF16)
    return pl.pallas_call(
        functools.partial(_even_in_kernel, n_x=len(x_args)),
        grid=(T_ALL // IN_TILE,),
        in_specs=x_specs + [
            _mod_row_spec(layer, tiles_per_seq=SEQ // IN_TILE), _const_spec((1, D)), _layer_spec(w_in_all, j),
            _const_spec((1, A_W)), _layer_spec(w_sp_all, j), _const_spec((A_GROUPS, CHUNK, 1)),
        ],
        out_specs=[tok(A_W)] * 4,
        out_shape=[out] * 4,
        compiler_params=_params(("arbitrary",)),
    )(*x_args, mod3, ng, w_in_all, lng, w_sp_all, b_sp)


def _attn_kernel(q_ref, k_ref, v_ref, kc_ref, vc_ref, tbl_ref, o_ref, *, ctx_queries):
    r = pl.program_id(1)
    lane = lax.broadcasted_iota(jnp.int32, (Q_BLK, 2 * HEAD_DIM), 1)
    first_head = lane < HEAD_DIM

    def attend(local):
        if local:
            start = pl.multiple_of(
                jnp.clip(r * Q_ROWS - WIN_H // 2, 0, ROWS - K_ROWS) * GRID_W, GRID_W)

        def pair_scores(hp):
            cols = slice(hp * 2 * HEAD_DIM, (hp + 1) * 2 * HEAD_DIM)
            q2 = q_ref[:, cols]
            k2 = kc_ref[:, cols]
            if local:
                k2 = jnp.concatenate([k_ref[pl.ds(start, K_BLK), cols], k2], axis=0)
            zero = jnp.zeros_like(q2)
            return _dot_nt(jnp.concatenate([jnp.where(first_head, q2, zero), jnp.where(first_head, zero, q2)],
                                           axis=0), k2)

        def pair_output(hp, s2):
            cols = slice(hp * 2 * HEAD_DIM, (hp + 1) * 2 * HEAD_DIM)
            v2 = vc_ref[:, cols]
            if local:
                v2 = jnp.concatenate([v_ref[pl.ds(start, K_BLK), cols], v2], axis=0)
            ps, dens = [], []
            for half in range(2):
                s = s2[half * Q_BLK:(half + 1) * Q_BLK]
                if local:
                    s = jnp.concatenate([s[:, 0:K_BLK] + tbl_ref[2 * hp + half], s[:, K_BLK:]], axis=1)
                m = jnp.max(s, axis=-1, keepdims=True)
                p = jnp.exp2(s - m)
                dens.append(jnp.sum(p, axis=-1, keepdims=True))
                ps.append(p.astype(BF16))
            o2 = _dot(jnp.concatenate(ps, axis=0), v2)
            o_ref[:, cols] = jnp.where(first_head, o2[0:Q_BLK] / dens[0], o2[Q_BLK:] / dens[1]).astype(BF16)

        s2 = pair_scores(0)
        for hp in range(HEADS // 2):
            s2_next = pair_scores(hp + 1) if hp + 1 < HEADS // 2 else None
            pair_output(hp, s2)
            s2 = s2_next

    if ctx_queries:
        pl.when(r < ROW_STEPS)(lambda: attend(True))
        pl.when(r >= ROW_STEPS)(lambda: attend(False))
    else:
        attend(True)


def _attention(q, k, v, tbl, ctx_queries):
    steps = ROW_STEPS + (1 if ctx_queries else 0)
    lat_blocks = T_LAT // Q_BLK

    def q_idx(b, r):
        return (jnp.where(r < ROW_STEPS, b * ROW_STEPS + r, lat_blocks + b), 0)

    def tbl_idx(b, r):
        return (jnp.where(r == 0, 0, jnp.where(r >= ROW_STEPS - 1, 2, 1)), 0, 0, 0)

    ctx_blk = T_LAT // CTX_LEN
    return pl.pallas_call(
        functools.partial(_attn_kernel, ctx_queries=ctx_queries),
        grid=(BATCH, steps),
        in_specs=[
            pl.BlockSpec((Q_BLK, B_W), q_idx),
            pl.BlockSpec((SEQ, B_W), lambda b, r: (b, 0)),
            pl.BlockSpec((SEQ, B_W), lambda b, r: (b, 0)),
            pl.BlockSpec((CTX_LEN, B_W), lambda b, r: (ctx_blk + b, 0)),
            pl.BlockSpec((CTX_LEN, B_W), lambda b, r: (ctx_blk + b, 0)),
            pl.BlockSpec((None, HEADS, Q_BLK, K_BLK), tbl_idx),
        ],
        out_specs=pl.BlockSpec((Q_BLK, B_W), q_idx),
        out_shape=jax.ShapeDtypeStruct((T_ALL if ctx_queries else T_LAT, B_W), BF16),
        compiler_params=_params(("arbitrary", "arbitrary")),
    )(q, k, v, k, v, tbl)


N_DR = 2 * WIN_H - 1
N_DC = 2 * WIN_W - 1
TBL_R0 = (0, Q_ROWS, ROWS - Q_ROWS)


def _tbl_kernel(rpb_ref, o_ref, t_scr):
    h = pl.program_id(0)
    shape = (GRID_W, 2 * GRID_W)
    cq = lax.broadcasted_iota(jnp.int32, shape, 0)
    lane = lax.broadcasted_iota(jnp.int32, shape, 1)
    ck = lane & (GRID_W - 1)
    dc = jnp.clip(ck - cq + WIN_W - 1, 0, N_DC - 1)
    cs = jnp.clip(cq - WIN_W // 2, 0, GRID_W - WIN_W)
    col_ok = (ck >= cs) & (ck < cs + WIN_W)
    neg = jnp.full(shape, NEG, F32)
    for dr in range(N_DR):
        t = neg
        for j in range(N_DC):
            t = jnp.where(dc == j, rpb_ref[(h * N_DR + dr) * N_DC + j], t)
        t_scr[dr] = jnp.where(col_ok, t * LOG2E, NEG)
    left = lane < GRID_W
    for v, r0 in enumerate(TBL_R0):
        k0 = min(max(r0 - WIN_H // 2, 0), ROWS - K_ROWS)
        for qi in range(Q_ROWS):
            qrow = r0 + qi
            w0 = min(max(qrow - WIN_H // 2, 0), ROWS - WIN_H)
            for kp in range(K_ROWS // 2):
                halves = []
                for krow in (k0 + 2 * kp, k0 + 2 * kp + 1):
                    in_win = w0 <= krow < w0 + WIN_H
                    halves.append(t_scr[krow - qrow + WIN_H - 1] if in_win else neg)
                o_ref[v, qi * GRID_W:(qi + 1) * GRID_W, kp * 2 * GRID_W:(kp + 1) * 2 * GRID_W] = (
                    jnp.where(left, halves[0], halves[1]))


def _bias_tables(rpb):
    return pl.pallas_call(
        _tbl_kernel,
        grid=(HEADS,),
        in_specs=[pl.BlockSpec(memory_space=pltpu.SMEM)],
        out_specs=pl.BlockSpec((len(TBL_R0), None, Q_BLK, K_BLK), lambda h: (0, h, 0, 0)),
        out_shape=jax.ShapeDtypeStruct((len(TBL_R0), HEADS, Q_BLK, K_BLK), F32),
        scratch_shapes=[pltpu.VMEM((N_DR, GRID_W, 2 * GRID_W), F32)],
        compiler_params=_params(("arbitrary",)),
    )(rpb.reshape(-1))


def _pin_after(value, anchor):
    bits = lambda t: pltpu.bitcast(t, jnp.uint32)
    zero = lax.shift_right_logical(lax.shift_right_logical(bits(anchor), jnp.uint32(16)), jnp.uint32(16))
    return pltpu.bitcast(bits(value) | zero, F32)


def _ffn_tail(x, y, mod_ref, ng2_ref, w_all, first_blk, fg_ref, side_work=None, h2_scr=None):
    x1 = x + mod_ref[:, 2 * D:3 * D] * y
    h2f = _rms_mod(x1, ng2_ref[...], mod_ref[:, 3 * D:4 * D], mod_ref[:, 4 * D:5 * D])
    h2 = h2f.astype(BF16)
    if side_work is not None:
        h2_scr[...] = h2
    acc = None
    token = None
    for c in range(FFN_CHUNKS):
        if side_work is not None:
            if token is not None:
                h2_scr[0:2 * SUBLANES, 0:LANES] = _pin_after(h2f[0:2 * SUBLANES, 0:LANES], token).astype(BF16)
                h2 = h2_scr[...]
            token = side_work(c, y if c == 0 else acc)
        hid = jnp.maximum(_dot(h2, w_all[first_blk + c]), 0.0)
        part = _dot((hid * hid).astype(BF16), w_all[first_blk + FFN_CHUNKS + c])
        acc = part if acc is None else acc + part
    x2 = x1 + mod_ref[:, 5 * D:6 * D] * acc
    if fg_ref is not None:
        x2 = x2 * lax.rsqrt(jnp.mean(x2 * x2, axis=-1, keepdims=True) + EPS) * fg_ref[...]
    return x2


IN_HBM = pl.BlockSpec(memory_space=pl.ANY)


def _ffn_blocks(w1_hbm, w2_hbm, layer):
    return ([w1_hbm.at[layer, :, pl.ds(c * WBLK, WBLK)] for c in range(FFN_CHUNKS)]
            + [w2_hbm.at[layer, pl.ds(c * WBLK, WBLK), :] for c in range(FFN_CHUNKS)])


def _stage_weights(srcs, w_all, stage, sem):
    copies = [pltpu.make_async_copy(src, stage.at[n % 2], sem.at[n % 2]) for n, src in enumerate(srcs)]
    copies[0].start()
    for n in range(len(srcs)):
        if n + 1 < len(srcs):
            copies[n + 1].start()
        copies[n].wait()
        w_all[n] = stage[n % 2].astype(BF16)


def _even_out_kernel(*refs, final, n_x, layer, j):
    x_refs, refs = refs[:n_x], refs[n_x:]
    a_ref, b_ref, mod_ref, ng2_ref, wo_hbm, w1_hbm, w2_hbm = refs[:7]
    fg_ref = refs[7] if final else None
    o_ref, w_all, stage, w_sem = refs[-4:]

    @pl.when(pl.program_id(0) == 0)
    def _():
        _stage_weights([wo_hbm.at[j]] + _ffn_blocks(w1_hbm, w2_hbm, layer), w_all, stage, w_sem)

    y = _dot(a_ref[...], w_all[0, 0:A_W, :]) + _dot(b_ref[...], w_all[0, A_W:A_W + B_W, :])
    o_ref[...] = _ffn_tail(_load_stream(x_refs, T_LAT // OUT_TILE), y, mod_ref, ng2_ref, w_all, 1, fg_ref)


def _even_out(a, b, xs, mod3, layer, ng2, w_out_all, j, w1_all, w2_all, n_out, final_g=None):
    x_specs, x_args = _stream_specs(xs, OUT_TILE)
    tok = lambda w: pl.BlockSpec((OUT_TILE, w), lambda i: (i, 0))
    in_specs = x_specs + [tok(A_W), tok(B_W), _mod_row_spec(layer, tiles_per_seq=SEQ // OUT_TILE),
                          _const_spec((1, D)), IN_HBM, IN_HBM, IN_HBM]
    args = x_args + [a, b, mod3, ng2, w_out_all, w1_all, w2_all]
    if final_g is not None:
        in_specs.append(_const_spec((1, D)))
        args.append(final_g)
    return pl.pallas_call(
        functools.partial(_even_out_kernel, final=final_g is not None, n_x=len(x_args), layer=layer, j=j),
        grid=(n_out // OUT_TILE,),
        in_specs=in_specs,
        out_specs=tok(D),
        out_shape=jax.ShapeDtypeStruct((n_out, D), F32),
        scratch_shapes=[pltpu.VMEM((1 + 2 * FFN_CHUNKS, WBLK, WBLK), BF16), pltpu.VMEM((2, WBLK, WBLK), F32),
                        pltpu.SemaphoreType.DMA((2,))],
        compiler_params=_params(("arbitrary",)),
    )(*args)


def _odd_kernel(*refs, final, n_tiles, layer, j):
    (xp_ref, xa_ref, xn_ref, xb_ref, moda_ref, modb_ref, ng1_ref, ng2_ref, wp1_hbm, bp1_ref, wdw_ref,
     bdw_ref, lg_ref, lb_ref, wp2_hbm, bp2_ref, w1_hbm, w2_hbm) = refs[:18]
    fg_ref = refs[18] if final else None
    o_ref, y_scr, conv_scr, w_scr, h2_scr, w_all, stage, w_sem = refs[-8:]
    i = pl.program_id(0)

    @pl.when(i == 0)
    def _():
        conv_scr[...] = jnp.zeros_like(conv_scr)
        _stage_weights([wp1_hbm.at[j, :, pl.ds(0, D)], wp1_hbm.at[j, :, pl.ds(D, D)], wp2_hbm.at[j]]
                       + _ffn_blocks(w1_hbm, w2_hbm, layer), w_all, stage, w_sem)

    ia = jnp.minimum(i, n_tiles - 1)
    is_lat = ia < LAT_TILES
    pos = ia % TILES_PER_SEQ
    seq_first = jnp.logical_or(jnp.logical_not(is_lat), pos == 0)
    seq_last = jnp.logical_or(jnp.logical_not(is_lat), pos == TILES_PER_SEQ - 1)
    xe = jnp.concatenate([xp_ref[...], xa_ref[...], xn_ref[...]], axis=0)
    h = _rms_mod(xe, ng1_ref[...], moda_ref[:, 0:D], moda_ref[:, D:2 * D]).astype(BF16)
    for c0 in range(0, D, PW1_COLS):
        cols = slice(c0, c0 + PW1_COLS)
        gate = _dot(h, w_all[1, :, cols]) + bp1_ref[:, D + c0:D + c0 + PW1_COLS]
        glu = (_dot(h, w_all[0, :, cols]) + bp1_ref[:, cols]) * jax.nn.sigmoid(gate)
        for k in range(PW1_COLS // LANES):
            cb, lanes = c0 // LANES + k, slice(k * LANES, (k + 1) * LANES)
            y_scr[cb, 0:HALO] = jnp.where(seq_first, 0.0, glu[0:HALO, lanes])
            y_scr[cb, HALO:HALO + TILE] = glu[HALO:HALO + TILE, lanes]
            y_scr[cb, HALO + TILE:] = jnp.where(seq_last, 0.0, glu[HALO + TILE:, lanes])

    conv_rows = lambda: jnp.concatenate([conv_scr[cb] for cb in range(D // LANES)], axis=1)
    mu = jnp.mean(conv_rows(), axis=-1, keepdims=True)
    ac = conv_rows() - mu
    var = jnp.mean(ac * ac, axis=-1, keepdims=True)
    ac = conv_rows() - mu
    ln = ac * lax.rsqrt(var + EPS) * lg_ref[...] + lb_ref[...]
    act = (ln * jax.nn.sigmoid(ln)).astype(BF16)
    y = _dot(act, w_all[2]) + bp2_ref[...]

    base = HALO - CONV_K // 2
    rows_in = CONV_ROWS + SUBLANES

    row_blocks = TILE // CONV_ROWS

    def conv_piece(p, anchor):
        cb, r0 = p // row_blocks, (p % row_blocks) * CONV_ROWS
        lanes = slice(cb * LANES, (cb + 1) * LANES)
        w_scr[p, 0:CONV_K, :] = _pin_after(wdw_ref[:, lanes], anchor[0:1, 0:LANES])
        out = bdw_ref[:, lanes]
        for s in range(SUBLANES):
            part = None
            for off in range(s, base + CONV_K, SUBLANES):
                if off < base:
                    continue
                r1 = r0 + off - s
                term = y_scr[cb, r1:r1 + rows_in, :] * w_scr[p, off - base:off - base + 1, :]
                part = term if part is None else part + term
            out = out + part[s:s + CONV_ROWS]
        conv_scr[cb, r0:r0 + CONV_ROWS, :] = out
        return out[0:1, :]

    per_slot = CONV_PIECES // FFN_CHUNKS

    def conv_part(slot, anchor):
        for p in range(slot * per_slot, (slot + 1) * per_slot):
            token = conv_piece(p, anchor)
        return token

    o_ref[...] = _ffn_tail(xb_ref[...], y, modb_ref, ng2_ref, w_all, 3, fg_ref,
                           side_work=conv_part, h2_scr=h2_scr)


def _odd_layer(xs, mod3, layer, ng1, ng2, wp1_all, j, bp1, wdw, bdw, lg, lb, wp2_all, bp2, w1_all, w2_all,
               final_g=None):
    n = xs.shape[0]
    n_tiles = n // TILE
    per = TILE // HALO
    halo_blocks = n // HALO
    tile_a = lambda i: jnp.minimum(i, n_tiles - 1)
    tile_b = lambda i: jnp.maximum(i - 1, 0)
    in_specs = [
        pl.BlockSpec((HALO, D), lambda i: (jnp.maximum(tile_a(i) * per - 1, 0), 0)),
        pl.BlockSpec((TILE, D), lambda i: (tile_a(i), 0)),
        pl.BlockSpec((HALO, D), lambda i: (jnp.minimum((tile_a(i) + 1) * per, halo_blocks - 1), 0)),
        pl.BlockSpec((TILE, D), lambda i: (tile_b(i), 0)),
        _mod_row_spec(layer, tile_a), _mod_row_spec(layer, tile_b), _const_spec((1, D)), _const_spec((1, D)),
        IN_HBM, _const_spec((1, 2 * D)), _const_spec((CONV_K, D)), _const_spec((1, D)),
        _const_spec((1, D)), _const_spec((1, D)), IN_HBM, _const_spec((1, D)),
        IN_HBM, IN_HBM,
    ]
    args = [xs, xs, xs, xs, mod3, mod3, ng1, ng2, wp1_all, bp1, wdw, bdw, lg, lb, wp2_all, bp2, w1_all, w2_all]
    if final_g is not None:
        in_specs.append(_const_spec((1, D)))
        args.append(final_g)
    return pl.pallas_call(
        functools.partial(_odd_kernel, final=final_g is not None, n_tiles=n_tiles, layer=layer, j=j),
        grid=(n_tiles + 1,),
        in_specs=in_specs,
        out_specs=pl.BlockSpec((TILE, D), lambda i: (tile_b(i), 0)),
        out_shape=jax.ShapeDtypeStruct((n, D), F32),
        scratch_shapes=[pltpu.VMEM((D // LANES, HALO + TILE + HALO, LANES), F32),
                        pltpu.VMEM((D // LANES, TILE, LANES), F32),
                        pltpu.VMEM((CONV_PIECES, CONV_K + 1, LANES), F32), pltpu.VMEM((TILE, D), BF16),
                        pltpu.VMEM((3 + 2 * FFN_CHUNKS, WBLK, WBLK), BF16), pltpu.VMEM((2, WBLK, WBLK), F32),
                        pltpu.SemaphoreType.DMA((2,))],
        compiler_params=_params(("arbitrary",)),
    )(*args)


def kernel(x, c, ctx, c_ctx, w_mod, b_mod, norm_g, w_in, w_out, ln_v_g, w_sp, b_sp, rpb, w_pw1, b_pw1,
           w_dw, b_dw, ln_c_g, ln_c_b, w_pw2, b_pw2, w_ff1, w_ff2, final_g):
    row = lambda t: t.reshape(1, -1)
    cc = jnp.concatenate([c, c_ctx[None, :], jnp.zeros((MOD_ROWS - BATCH - 1, D), F32)], axis=0)
    mod3 = _modulation(cc, w_mod, b_mod).reshape(DEPTH * MOD_ROWS, 1, 6 * D)
    xs = (x.reshape(T_LAT, D), ctx.reshape(T_CTX, D))
    w_in, w_sp = w_in.astype(BF16), w_sp.astype(BF16)
    last_reader = ((DEPTH - 1) // 2) * 2
    for l in range(DEPTH):
        ctx_out = l < last_reader
        fg = row(final_g) if l == DEPTH - 1 else None
        j = l // 2
        if l % 2 == 0:
            a, q, k, v = _even_in(xs, mod3, l, row(norm_g[l, 0]), w_in, j, row(ln_v_g[j]),
                                  w_sp, b_sp[j][:, :, None])
            b = _attention(q, k, v, _bias_tables(rpb[j]), ctx_out)
            xs = _even_out(a, b, xs, mod3, l, row(norm_g[l, 1]), w_out, j, w_ff1, w_ff2,
                           T_ALL if ctx_out else T_LAT, fg)
        else:
            xs = _odd_layer(xs, mod3, l, row(norm_g[l, 0]), row(norm_g[l, 1]), w_pw1, j,
                            row(b_pw1[j]), w_dw[j], row(b_dw[j]), row(ln_c_g[j]), row(ln_c_b[j]),
                            w_pw2, row(b_pw2[j]), w_ff1, w_ff2, fg)
    return xs.reshape(BATCH, SEQ, D)
```
